```python
import math
import jax
import jax.numpy as jnp
from jax import lax
import numpy as np

D_MODEL = 1024
BATCH = 8
SEQ = 4096
DEPTH = 1
DEC_BATCH = 32
DEC_SEQ = 4
PAST_LEN = 16384
PAGE_SIZE = 128

N_META = 16
HG_HEADS = 4
HG_DK = 128
HG_DV = 128
HG_QK = HG_HEADS * HG_DK
HG_WIDTH = HG_HEADS * HG_DV
HG_CHUNK = 64
DA_HEADS = 4
DA_DH = 64
DA_DV = 2 * DA_DH
DA_QK = DA_HEADS * 2 * DA_DH
DA_WIDTH = DA_HEADS * DA_DV
MIX_WIDTH = HG_WIDTH + DA_WIDTH
Q_BLOCK = 128
ROPE_THETA = 10000.0
D_FF = 4 * D_MODEL
EPS = 1e-6
NEG_INF = -1e30
IN_SPLITS = (HG_QK, 2 * HG_QK, 2 * HG_QK + HG_WIDTH, 2 * HG_QK + 2 * HG_WIDTH,
             2 * HG_QK + 2 * HG_WIDTH + DA_QK, 2 * HG_QK + 2 * HG_WIDTH + 2 * DA_QK)
IN_WIDTH = 2 * HG_QK + 2 * HG_WIDTH + 2 * DA_QK + DA_WIDTH

kernel_name = "hymba_hgrn2_diffattn_step"


def _rmsnorm(x, g):
    xf = x.astype(jnp.float32)
    y = xf * lax.rsqrt(jnp.mean(xf * xf, axis=-1, keepdims=True) + EPS)
    return (y * g.astype(jnp.float32)).astype(x.dtype)


def _rope(x, pos):
    half = DA_DH // 2
    inv = jnp.power(ROPE_THETA, -jnp.arange(half, dtype=jnp.float32) * 2.0 / DA_DH)
    ang = pos.astype(jnp.float32)[:, None] * inv[None, :]
    cos = jnp.cos(ang)[None, :, None, None, :]
    sin = jnp.sin(ang)[None, :, None, None, :]
    xf = x.astype(jnp.float32)
    x1, x2 = xf[..., :half], xf[..., half:]
    return jnp.concatenate([x1 * cos - x2 * sin, x2 * cos + x1 * sin], axis=-1).astype(x.dtype)


def _hgrn2_chunk(S0, q, k, v, g):
    C = q.shape[1]
    b = jnp.cumsum(g, axis=1)
    causal = jnp.tril(jnp.ones((C, C), dtype=bool))
    diff = b[:, :, None] - b[:, None, :]
    decay = jnp.exp(jnp.where(causal[None, :, :, None, None], diff, NEG_INF))
    a = jnp.einsum("bthk,btshk,bshk->bhts", q, decay, k)
    o = jnp.einsum("bhts,bshv->bthv", a, v) + jnp.einsum("bthk,bhkv->bthv", q * jnp.exp(b), S0)
    b_last = b[:, -1]
    S = jnp.exp(b_last)[..., None] * S0 + jnp.einsum(
        "bshk,bshv->bhkv", k * jnp.exp(b_last[:, None] - b), v)
    return S, o


def _hgrn2_prompt(q, k, v, g):
    B, T = q.shape[:2]
    pad = HG_CHUNK - N_META
    padw = ((0, 0), (pad, 0), (0, 0), (0, 0))
    q, k, v, g = (jnp.pad(a, padw) for a in (q, k, v, g))
    n = q.shape[1] // HG_CHUNK

    def to_chunks(a):
        return a.reshape(B, n, HG_CHUNK, *a.shape[2:]).swapaxes(0, 1)

    S0 = jnp.zeros((B, HG_HEADS, HG_DK, HG_DV), jnp.float32)
    S, o = lax.scan(lambda S, c: _hgrn2_chunk(S, *c), S0,
                    (to_chunks(q), to_chunks(k), to_chunks(v), to_chunks(g)))
    o = o.swapaxes(0, 1).reshape(B, n * HG_CHUNK, HG_HEADS, HG_DV)[:, pad:]
    return o, S


def _diff_attn_core(q, k, v, lam, q_pos, k_pos):
    s = jnp.einsum("bqhmd,bkhmd->bhmqk", q, k).astype(jnp.float32) * (DA_DH ** -0.5)
    mask = k_pos[None, :] <= q_pos[:, None]
    p = jax.nn.softmax(jnp.where(mask, s, NEG_INF), axis=-1)
    w = p[:, :, 0] - lam * p[:, :, 1]
    return jnp.einsum("bhqk,bkhe->bqhe", w.astype(v.dtype), v)


def _diff_attn_prompt(q, k, v, lam, pos):
    B, T = q.shape[:2]
    nb = -(-T // Q_BLOCK)
    tp = nb * Q_BLOCK
    qb = jnp.pad(q, ((0, 0), (0, tp - T), (0, 0), (0, 0), (0, 0)))
    qb = qb.reshape(B, nb, Q_BLOCK, DA_HEADS, 2, DA_DH).swapaxes(0, 1)
    pb = jnp.arange(tp, dtype=pos.dtype).reshape(nb, Q_BLOCK)
    o = lax.map(lambda a: _diff_attn_core(a[0], k, v, lam, a[1], pos), (qb, pb))
    return o.swapaxes(0, 1).reshape(B, tp, DA_HEADS, DA_DV)[:, :T]


def _layer(x, pos, l, hgrn_fn, attn_fn, ln_mix, w_in, hg_lb_logits, hg_norm,
           lq1, lk1, lq2, lk2, da_subln, w_out, ln_ffn, w_up, w_down):
    B, T, _ = x.shape
    f32 = jnp.float32
    h = _rmsnorm(x, ln_mix[l])
    hq, hf, hi, hg, dq, dk, dv = jnp.split(h @ w_in[l], IN_SPLITS, axis=-1)
    lb = jnp.cumsum(jax.nn.softmax(hg_lb_logits.astype(f32), axis=0), axis=0)[l]
    f = lb + (1.0 - lb) * jax.nn.sigmoid(hf.astype(f32))
    shp = (B, T, HG_HEADS, HG_DK)
    q_h = jax.nn.silu(hq.astype(f32)).reshape(shp)
    k_h = (1.0 - f).reshape(shp)
    g_h = jnp.log(f).reshape(shp)
    v_h = hi.astype(f32).reshape(B, T, HG_HEADS, HG_DV)
    o_hg, S_new = hgrn_fn(q_h, k_h, v_h, g_h)
    o_hg = _rmsnorm(o_hg, hg_norm[l]) * jax.nn.silu(hg.astype(f32)).reshape(B, T, HG_HEADS, HG_DV)
    qa = _rope(dq.reshape(B, T, DA_HEADS, 2, DA_DH), pos)
    ka = _rope(dk.reshape(B, T, DA_HEADS, 2, DA_DH), pos)
    va = dv.reshape(B, T, DA_HEADS, DA_DV)
    lam_init = 0.8 - 0.6 * math.exp(-0.3 * l)
    lam = (jnp.exp(jnp.sum(lq1[l].astype(f32) * lk1[l].astype(f32)))
           - jnp.exp(jnp.sum(lq2[l].astype(f32) * lk2[l].astype(f32))) + lam_init)
    o_da = attn_fn(qa, ka, va, lam)
    o_da = _rmsnorm(o_da, da_subln[l]).astype(f32) * (1.0 - lam_init)
    mix = jnp.concatenate([o_hg.reshape(B, T, HG_WIDTH), o_da.reshape(B, T, DA_WIDTH)], axis=-1)
    x = x + mix.astype(x.dtype) @ w_out[l]
    u = jax.nn.relu(_rmsnorm(x, ln_ffn[l]) @ w_up[l])
    x = x + (u * u) @ w_down[l]
    return x, ka, va, S_new


def setup_inputs(seed: int = 0) -> dict:
    key = jax.random.key(seed)
    ks = jax.random.split(key, 24)
    n_pages = PAST_LEN // PAGE_SIZE
    n_used = DEC_BATCH * n_pages
    n_pool = (5 * n_used + 3) // 4
    f32 = jnp.float32

    def nrm(k, shape, scale=1.0):
        return jax.random.normal(k, shape, f32) * scale

    def gain(k, shape):
        return 1.0 + nrm(k, shape, 0.01)

    perm = jax.random.permutation(ks[5], n_pool)
    page_table = perm[:n_used].reshape(DEC_BATCH, n_pages).astype(jnp.int32)
    return {
        "x_prompt": nrm(ks[0], (BATCH, SEQ, D_MODEL)),
        "x_sample": nrm(ks[1], (DEC_BATCH, DEC_SEQ, D_MODEL)),
        "cache_k": nrm(ks[2], (DEPTH, n_pool, PAGE_SIZE, DA_HEADS, 2, DA_DH)),
        "cache_v": nrm(ks[3], (DEPTH, n_pool, PAGE_SIZE, DA_HEADS, DA_DV)),
        "state_hgrn": nrm(ks[4], (DEPTH, DEC_BATCH, HG_HEADS, HG_DK, HG_DV), 0.3),
        "page_table": page_table,
        "meta_tokens": nrm(ks[6], (N_META, D_MODEL)),
        "ln_mix": gain(ks[7], (DEPTH, D_MODEL)),
        "w_in": nrm(ks[8], (DEPTH, D_MODEL, IN_WIDTH), D_MODEL ** -0.5),
        "hg_lb_logits": nrm(ks[9], (DEPTH + 1, HG_QK), 0.1),
        "hg_norm": gain(ks[10], (DEPTH, HG_DV)),
        "da_lambda_q1": nrm(ks[11], (DEPTH, DA_DH), 0.1),
        "da_lambda_k1": nrm(ks[12], (DEPTH, DA_DH), 0.1),
        "da_lambda_q2": nrm(ks[13], (DEPTH, DA_DH), 0.1),
        "da_lambda_k2": nrm(ks[14], (DEPTH, DA_DH), 0.1),
        "da_subln": gain(ks[15], (DEPTH, DA_DV)),
        "w_out": nrm(ks[16], (DEPTH, MIX_WIDTH, D_MODEL), MIX_WIDTH ** -0.5),
        "ln_ffn": gain(ks[17], (DEPTH, D_MODEL)),
        "w_up": nrm(ks[18], (DEPTH, D_MODEL, D_FF), D_MODEL ** -0.5),
        "w_down": nrm(ks[19], (DEPTH, D_FF, D_MODEL), D_FF ** -0.5),
        "ln_final": gain(ks[20], (D_MODEL,)),
    }


def reference(x_prompt, x_sample, cache_k, cache_v, state_hgrn, page_table, meta_tokens,
              ln_mix, w_in, hg_lb_logits, hg_norm, da_lambda_q1, da_lambda_k1, da_lambda_q2,
              da_lambda_k2, da_subln, w_out, ln_ffn, w_up, w_down, ln_final):
    B = x_prompt.shape[0]
    meta = jnp.broadcast_to(meta_tokens[None].astype(x_prompt.dtype), (B, N_META, x_prompt.shape[-1]))
    xp = jnp.concatenate([meta, x_prompt], axis=1)
    pos_p = jnp.arange(xp.shape[1], dtype=jnp.int32)
    xs = x_sample
    nb_s, ds = xs.shape[0], xs.shape[1]
    past_len = page_table.shape[1] * PAGE_SIZE
    pos_s = past_len + jnp.arange(ds, dtype=jnp.int32)
    k_pos_s = jnp.arange(past_len + ds, dtype=jnp.int32)
    kp_l, vp_l, sp_l, ks_l, vs_l, ss_l = [], [], [], [], [], []
    for l in range(DEPTH):
        wts = (ln_mix, w_in, hg_lb_logits, hg_norm, da_lambda_q1, da_lambda_k1, da_lambda_q2,
               da_lambda_k2, da_subln, w_out, ln_ffn, w_up, w_down)
        xp, kp, vp, sp = _layer(
            xp, pos_p, l, _hgrn2_prompt,
            lambda q, k, v, lam: _diff_attn_prompt(q, k, v, lam, pos_p), *wts)

        def hgrn_sample(q, k, v, g):
            S, o = _hgrn2_chunk(state_hgrn[l].astype(jnp.float32), q, k, v, g)
            return o, S

        def attn_sample(q, k, v, lam):
            kc = cache_k[l, page_table].reshape(nb_s, past_len, DA_HEADS, 2, DA_DH)
            vc = cache_v[l, page_table].reshape(nb_s, past_len, DA_HEADS, DA_DV)
            k_all = jnp.concatenate([kc.astype(k.dtype), k], axis=1)
            v_all = jnp.concatenate([vc.astype(v.dtype), v], axis=1)
            return _diff_attn_core(q, k_all, v_all, lam, pos_s, k_pos_s)

        xs, ksn, vsn, ssn = _layer(xs, pos_s, l, hgrn_sample, attn_sample, *wts)
        kp_l.append(kp); vp_l.append(vp); sp_l.append(sp)
        ks_l.append(ksn); vs_l.append(vsn); ss_l.append(ssn)
    y_prompt = _rmsnorm(xp, ln_final)[:, N_META:]
    y_sample = _rmsnorm(xs, ln_final)
    k_prompt = jnp.stack(kp_l)
    v_prompt = jnp.stack(vp_l)
    state_prompt = jnp.stack(sp_l)
    k_sample = jnp.stack(ks_l)
    v_sample = jnp.stack(vs_l)
    state_sample = jnp.stack(ss_l)
    return (y_prompt, y_sample, k_prompt, v_prompt, state_prompt, k_sample, v_sample, state_sample)
```

```python
import functools
import math

import jax
import jax.numpy as jnp
import numpy as np
from jax import lax
from jax.experimental import pallas as pl
from jax.experimental.pallas import tpu as pltpu

F32 = jnp.float32
BF16 = jnp.bfloat16

D_MODEL = 1024
N_META = 16
HEADS = 4
HEAD_W = 128
GROUP_W = HEADS * HEAD_W
DA_DH = 64
PAGE_SIZE = 128
ROPE_THETA = 10000.0
EPS = 1e-6
NEG_INF = -1e30
N_SEG = 7

VMEM_LIMIT = 56 * 1024 * 1024

NT = (((1,), (1,)), ((), ()))
TN = (((0,), (0,)), ((), ()))


def _dot(a, b, dims=None):
    if dims is None:
        return jnp.dot(a, b, preferred_element_type=F32)
    return lax.dot_general(a, b, dims, preferred_element_type=F32)


def _rms(x):
    return x * lax.rsqrt(jnp.mean(x * x, axis=-1, keepdims=True) + EPS)


def _const_spec(shape):
    nd = len(shape)
    return pl.BlockSpec(shape, lambda *_: (0,) * nd, pipeline_mode=pl.Buffered(1))


def _inproj_kernel(x_ref, cos_ref, sin_ref, ln_ref, w_ref, lb_ref,
                   qh_ref, f_ref, vh_ref, gate_ref, qr_ref, k32_ref, k16_ref, v32_ref, v16_ref,
                   *, layer):
    x = x_ref[...]
    h = (_rms(x) * ln_ref[...]).astype(BF16)

    def seg(j):
        return _dot(h, w_ref[:, j * GROUP_W:(j + 1) * GROUP_W])

    lg = lb_ref[...]
    e = jnp.exp(lg - jnp.max(lg, axis=0, keepdims=True))
    sm = e / jnp.sum(e, axis=0, keepdims=True)
    lb = jnp.sum(sm[:layer + 1], axis=0, keepdims=True)

    qh_ref[...] = jax.nn.silu(seg(0)).astype(qh_ref.dtype)
    f_ref[...] = lb + (1.0 - lb) * jax.nn.sigmoid(seg(1))
    vh_ref[...] = seg(2).astype(vh_ref.dtype)
    gate_ref[...] = jax.nn.silu(seg(3)).astype(gate_ref.dtype)

    c4 = jnp.concatenate([cos_ref[...]] * HEADS, axis=1)
    s4 = jnp.concatenate([sin_ref[...]] * HEADS, axis=1)
    lane = lax.broadcasted_iota(jnp.int32, c4.shape, 1)
    first_half = (lane & (DA_DH // 2)) == 0

    def rope(y):
        swapped = jnp.where(first_half,
                            pltpu.roll(y, GROUP_W - DA_DH // 2, 1),
                            pltpu.roll(y, DA_DH // 2, 1))
        return y * c4 + swapped * s4

    qr_ref[...] = (rope(seg(4)) * (DA_DH ** -0.5)).astype(qr_ref.dtype)
    k = rope(seg(5))
    k32_ref[...] = k
    k16_ref[...] = k.astype(BF16)
    v = seg(6)
    v32_ref[...] = v
    v16_ref[...] = v.astype(BF16)


def _inproj(x, cos, sin, ln_mix, w_in, lb_logits, *, tm, layer):
    rows = x.shape[0]
    nper = cos.shape[0] // tm
    grid = (rows // tm,)
    row = lambda w: pl.BlockSpec((tm, w), lambda i: (i, 0))
    tab = pl.BlockSpec((tm, HEAD_W), lambda i: (i % nper, 0))
    outs = [(BF16,), (F32,), (BF16,), (BF16,), (BF16,), (F32,), (BF16,), (F32,), (BF16,)]
    return pl.pallas_call(
        functools.partial(_inproj_kernel, layer=layer),
        grid=grid,
        in_specs=[row(D_MODEL), tab, tab, _const_spec((1, D_MODEL)),
                  _const_spec(w_in.shape), _const_spec(lb_logits.shape)],
        out_specs=[row(GROUP_W)] * len(outs),
        out_shape=[jax.ShapeDtypeStruct((rows, GROUP_W), d[0]) for d in outs],
        compiler_params=pltpu.CompilerParams(dimension_semantics=("parallel",),
                                             vmem_limit_bytes=VMEM_LIMIT),
        name="inproj",
    )(x, cos, sin, ln_mix, w_in, lb_logits)


def _hgrn_weights(chunk):
    levels = int(math.log2(chunk))
    w = np.zeros(((2 + levels) * chunk, chunk), np.float32)
    for t in range(chunk):
        w[t, :t + 1] = 1.0
        w[chunk + t, t + 1:] = 1.0
    row, half = 2 * chunk, chunk // 2
    while half >= 1:
        for t in range(chunk):
            blk = t // half
            if blk % 2 == 1:
                w[row + t, blk * half:t + 1] = 1.0
            else:
                w[row + t, t + 1:blk * half + half] = 1.0
        row += chunk
        half //= 2
    return w


def _hgrn_kernel(q_ref, f_ref, v_ref, gate_ref, s0_ref, w_ref, norm_ref, o_ref, s_out_ref, st_scr,
                 *, chunk, n_chunks):
    i = pl.program_id(1)
    levels = int(math.log2(chunk))

    @pl.when(i == 0)
    def _():
        for h in range(HEADS):
            st_scr[h] = s0_ref[h].T

    row = lax.broadcasted_iota(jnp.int32, (chunk, chunk), 0)
    col = lax.broadcasted_iota(jnp.int32, (chunk, chunk), 1)
    rowv = lax.broadcasted_iota(jnp.int32, (chunk, GROUP_W), 0)
    eye = row == col
    pair_masks, second_half = [], []
    half = chunk // 2
    for _ in range(levels):
        sh = int(math.log2(half))
        pair_masks.append((row >> (sh + 1)) == (col >> (sh + 1)))
        second_half.append(((rowv >> sh) & 1) == 1)
        half //= 2
    w = w_ref[...]
    norm = norm_ref[...]

    def body(c, carry):
        r0 = pl.multiple_of(c * chunk, chunk)
        rows = pl.ds(r0, chunk)
        q = q_ref[rows, :].astype(F32)
        f = f_ref[rows, :]
        v = v_ref[rows, :].astype(BF16)
        gate = gate_ref[rows, :].astype(F32)
        g = jnp.log(f)
        k = 1.0 - f
        g_hi = g.astype(BF16)
        g_lo = (g - g_hi.astype(F32)).astype(BF16)
        ex = _dot(w, g_hi) + _dot(w, g_lo)
        eb = jnp.exp(ex[0:chunk])
        qe = (q * eb).astype(BF16)
        kk = (k * jnp.exp(ex[chunk:2 * chunk])).astype(BF16)
        qb = q.astype(BF16)
        kb = k.astype(BF16)
        qt, kt = [], []
        for l in range(levels):
            dec = jnp.exp(ex[(2 + l) * chunk:(3 + l) * chunk])
            qt.append(jnp.where(second_half[l], q * dec, 0.0).astype(BF16))
            kt.append(jnp.where(second_half[l], 0.0, k * dec).astype(BF16))
        eb_last = eb[chunk - 1:chunk, :]
        for h in range(HEADS):
            sl = slice(h * HEAD_W, (h + 1) * HEAD_W)
            a = jnp.where(eye, _dot(qb[:, sl], kb[:, sl], NT), 0.0)
            for l in range(levels):
                a = a + jnp.where(pair_masks[l], _dot(qt[l][:, sl], kt[l][:, sl], NT), 0.0)
            st = st_scr[h]
            o = _dot(a.astype(BF16), v[:, sl]) + _dot(qe[:, sl], st.astype(BF16), NT)
            st_scr[h] = st * eb_last[:, sl] + _dot(v[:, sl], kk[:, sl], TN)
            o = _rms(o) * norm * gate[:, sl]
            o_ref[rows, sl] = o.astype(o_ref.dtype)
        return carry

    lax.fori_loop(0, n_chunks, body, 0)

    @pl.when(i == pl.num_programs(1) - 1)
    def _():
        for h in range(HEADS):
            s_out_ref[h] = st_scr[h].T


def _hgrn(q, f, v, gate, s0, hg_norm, *, nb, chunk, tm, out_dtype):
    rows = q.shape[0]
    nblk = rows // nb // tm
    w = jnp.asarray(_hgrn_weights(chunk), BF16)
    per_batch_state = s0.shape[0] == nb
    row = pl.BlockSpec((tm, GROUP_W), lambda b, i: (b * nblk + i, 0))
    st_in = pl.BlockSpec((None, HEADS, HEAD_W, HEAD_W),
                         (lambda b, i: (b, 0, 0, 0)) if per_batch_state else (lambda b, i: (0, 0, 0, 0)))
    st_out = pl.BlockSpec((None, HEADS, HEAD_W, HEAD_W), lambda b, i: (b, 0, 0, 0))
    return pl.pallas_call(
        functools.partial(_hgrn_kernel, chunk=chunk, n_chunks=tm // chunk),
        grid=(nb, nblk),
        in_specs=[row, row, row, row, st_in, _const_spec(w.shape), _const_spec((1, HEAD_W))],
        out_specs=[row, st_out],
        out_shape=[jax.ShapeDtypeStruct((rows, GROUP_W), out_dtype),
                   jax.ShapeDtypeStruct((nb, HEADS, HEAD_W, HEAD_W), F32)],
        scratch_shapes=[pltpu.VMEM((HEADS, HEAD_W, HEAD_W), F32)],
        compiler_params=pltpu.CompilerParams(dimension_semantics=("parallel", "arbitrary"),
                                             vmem_limit_bytes=VMEM_LIMIT),
        name=f"hgrn_c{chunk}",
    )(q, f, v, gate, s0, w, hg_norm)


def _lambda(lq1, lk1, lq2, lk2, lam_init):
    return (jnp.exp(jnp.sum(lq1 * lk1, axis=-1, keepdims=True))
            - jnp.exp(jnp.sum(lq2 * lk2, axis=-1, keepdims=True)) + lam_init)


def _softmax_step(s, v, m_ref, l_ref, acc_ref):
    m_prev = m_ref[...]
    m_new = jnp.maximum(m_prev, jnp.max(s, axis=-1, keepdims=True))
    alpha = jnp.exp(m_prev - m_new)
    p = jnp.exp(s - m_new[:, :1])
    l_ref[...] = alpha * l_ref[...] + jnp.sum(p, axis=-1, keepdims=True)
    acc_ref[...] = alpha * acc_ref[...] + _dot(p.astype(BF16), v)
    m_ref[...] = m_new


def _split_maps(q):
    lane = lax.broadcasted_iota(jnp.int32, q.shape, 1)
    zero = jnp.zeros_like(q)
    return jnp.concatenate([jnp.where(lane < DA_DH, q, zero), jnp.where(lane >= DA_DH, q, zero)], axis=0)


def _attn_kernel(q_ref, k_ref, v_ref, km_ref, vm_ref, lq1_ref, lk1_ref, lq2_ref, lk2_ref, subln_ref,
                 o_ref, m_scr, l_scr, acc_scr, *, tq, lam_init):
    qi = pl.program_id(2)
    qbd = _split_maps(q_ref[...])
    col = lax.broadcasted_iota(jnp.int32, (2 * tq, HEAD_W), 1)

    m_scr[...] = jnp.full(m_scr.shape, NEG_INF, F32)
    l_scr[...] = jnp.zeros(l_scr.shape, F32)
    acc_scr[...] = jnp.zeros(acc_scr.shape, F32)

    s = _dot(qbd, km_ref[...], NT)
    _softmax_step(jnp.where(col < N_META, s, NEG_INF), vm_ref[...], m_scr, l_scr, acc_scr)

    def body(j, carry):
        rows = pl.ds(pl.multiple_of(j * tq, tq), tq)
        _softmax_step(_dot(qbd, k_ref[rows, :], NT), v_ref[rows, :], m_scr, l_scr, acc_scr)
        return carry

    lax.fori_loop(0, qi, body, 0)

    rows = pl.ds(pl.multiple_of(qi * tq, tq), tq)
    s = _dot(qbd, k_ref[rows, :], NT)
    r = lax.broadcasted_iota(jnp.int32, s.shape, 0)
    c = lax.broadcasted_iota(jnp.int32, s.shape, 1)
    r = jnp.where(r >= tq, r - tq, r)
    _softmax_step(jnp.where(c <= r, s, NEG_INF), v_ref[rows, :], m_scr, l_scr, acc_scr)

    lam = _lambda(lq1_ref[...], lk1_ref[...], lq2_ref[...], lk2_ref[...], lam_init)
    out = acc_scr[...] / l_scr[...]
    o = out[:tq] - lam * out[tq:]
    o = _rms(o) * subln_ref[...] * (1.0 - lam_init)
    o_ref[...] = o.astype(o_ref.dtype)


def _attn(q, k, v, k_meta, v_meta, lams, subln, *, nb, seq, tq, lam_init):
    nq = seq // tq
    qspec = pl.BlockSpec((tq, HEAD_W), lambda b, h, i: (b * nq + i, h))
    kvspec = pl.BlockSpec((seq, HEAD_W), lambda b, h, i: (b, h))
    mspec = pl.BlockSpec((HEAD_W, HEAD_W), lambda b, h, i: (0, h))
    vec = lambda n: pl.BlockSpec((1, n), lambda b, h, i: (0, 0))
    return pl.pallas_call(
        functools.partial(_attn_kernel, tq=tq, lam_init=lam_init),
        grid=(nb, HEADS, nq),
        in_specs=[qspec, kvspec, kvspec, mspec, mspec, vec(DA_DH), vec(DA_DH), vec(DA_DH), vec(DA_DH),
                  vec(HEAD_W)],
        out_specs=qspec,
        out_shape=jax.ShapeDtypeStruct((nb * seq, GROUP_W), BF16),
        scratch_shapes=[pltpu.VMEM((2 * tq, HEAD_W), F32)] * 3,
        compiler_params=pltpu.CompilerParams(dimension_semantics=("parallel", "parallel", "arbitrary"),
                                             vmem_limit_bytes=VMEM_LIMIT),
        name="prompt_attn",
    )(q, k, v, k_meta, v_meta, *lams, subln)


def _decode_kernel(pt_ref, q_ref, kn_ref, vn_ref, lq1_ref, lk1_ref, lq2_ref, lk2_ref, subln_ref, *refs,
                   pages, n_new, lam_init):
    del pt_ref
    k_pages, v_pages = refs[:pages], refs[pages:2 * pages]
    o_ref, m_scr, l_scr, acc_scr = refs[2 * pages:]
    s_idx = pl.program_id(1)
    rows = 2 * n_new

    @pl.when(s_idx == 0)
    def _():
        m_scr[...] = jnp.full(m_scr.shape, NEG_INF, F32)
        l_scr[...] = jnp.zeros(l_scr.shape, F32)
        acc_scr[...] = jnp.zeros(acc_scr.shape, F32)

    for h in range(HEADS):
        sl = slice(h * HEAD_W, (h + 1) * HEAD_W)
        kh = jnp.concatenate([r[sl, :] for r in k_pages], axis=1).astype(BF16)
        vh = jnp.concatenate([r[:, h, :] for r in v_pages], axis=0).astype(BF16)
        _softmax_step(_dot(q_ref[h], kh), vh, m_scr.at[h], l_scr.at[h], acc_scr.at[h])

    @pl.when(s_idx == pl.num_programs(1) - 1)
    def _():
        lam = _lambda(lq1_ref[...], lk1_ref[...], lq2_ref[...], lk2_ref[...], lam_init)
        r = lax.broadcasted_iota(jnp.int32, (rows, HEAD_W), 0)
        c = lax.broadcasted_iota(jnp.int32, (rows, HEAD_W), 1)
        r = jnp.where(r >= n_new, r - n_new, r)
        visible = c <= r
        for h in range(HEADS):
            sl = slice(h * HEAD_W, (h + 1) * HEAD_W)
            s = _dot(q_ref[h], kn_ref[:, sl], NT)
            _softmax_step(jnp.where(visible, s, NEG_INF), vn_ref[:, sl], m_scr.at[h], l_scr.at[h], acc_scr.at[h])
            out = acc_scr[h] / l_scr[h]
            o = out - lam * pltpu.roll(out, rows - n_new, 0)
            o = _rms(o) * subln_ref[...] * (1.0 - lam_init)
            o_ref[:, sl] = o


def _decode_attn(page_table, q_split, k_new, v_new, cache_k, cache_v, lams, subln, *, pages, n_new, lam_init):
    nb, n_pages = page_table.shape
    steps = n_pages // pages
    rows = 2 * n_new
    per_b = lambda shape: pl.BlockSpec((None,) + shape, lambda b, s, pt: (b,) + (0,) * len(shape))
    vec = lambda n: pl.BlockSpec((1, n), lambda b, s, pt: (0, 0))

    def k_spec(p):
        return pl.BlockSpec((None, GROUP_W, PAGE_SIZE), lambda b, s, pt: (pt[b, s * pages + p], 0, 0))

    def v_spec(p):
        return pl.BlockSpec((None, PAGE_SIZE, HEADS, HEAD_W), lambda b, s, pt: (pt[b, s * pages + p], 0, 0, 0))

    grid_spec = pltpu.PrefetchScalarGridSpec(
        num_scalar_prefetch=1,
        grid=(nb, steps),
        in_specs=[per_b((HEADS, rows, HEAD_W)), per_b((PAGE_SIZE, GROUP_W)), per_b((PAGE_SIZE, GROUP_W)),
                  vec(DA_DH), vec(DA_DH), vec(DA_DH), vec(DA_DH), vec(HEAD_W)]
                 + [k_spec(p) for p in range(pages)] + [v_spec(p) for p in range(pages)],
        out_specs=per_b((rows, GROUP_W)),
        scratch_shapes=[pltpu.VMEM((HEADS, rows, HEAD_W), F32)] * 3,
    )
    return pl.pallas_call(
        functools.partial(_decode_kernel, pages=pages, n_new=n_new, lam_init=lam_init),
        grid_spec=grid_spec,
        out_shape=jax.ShapeDtypeStruct((nb, rows, GROUP_W), F32),
        compiler_params=pltpu.CompilerParams(dimension_semantics=("parallel", "arbitrary"),
                                             vmem_limit_bytes=VMEM_LIMIT),
        name="decode_attn",
    )(page_table, q_split, k_new, v_new, *lams, subln, *([cache_k] * pages), *([cache_v] * pages))


def _mlp_kernel(x_ref, ohg_ref, oda_ref, wo_ref, lnf_ref, wu_ref, wd_ref, lnfin_ref, y_ref, *, ff_chunk):
    x = x_ref[...]
    x = x + _dot(ohg_ref[...], wo_ref[0:GROUP_W, :]) + _dot(oda_ref[...], wo_ref[GROUP_W:2 * GROUP_W, :])
    hn = (_rms(x) * lnf_ref[...]).astype(BF16)
    acc = jnp.zeros_like(x)
    for c in range(wu_ref.shape[1] // ff_chunk):
        cs = slice(c * ff_chunk, (c + 1) * ff_chunk)
        u = jnp.maximum(_dot(hn, wu_ref[:, cs]), 0.0)
        acc = acc + _dot((u * u).astype(BF16), wd_ref[cs, :])
    y_ref[...] = _rms(x + acc) * lnfin_ref[...]


def _mlp(x, ohg, oda, w_out, ln_ffn, w_up, w_down, ln_final, *, tm, ff_chunk=512):
    rows = x.shape[0]
    row = lambda w: pl.BlockSpec((tm, w), lambda i: (i, 0))
    return pl.pallas_call(
        functools.partial(_mlp_kernel, ff_chunk=ff_chunk),
        grid=(rows // tm,),
        in_specs=[row(D_MODEL), row(GROUP_W), row(GROUP_W), _const_spec(w_out.shape), _const_spec((1, D_MODEL)),
                  _const_spec(w_up.shape), _const_spec(w_down.shape), _const_spec((1, D_MODEL))],
        out_specs=row(D_MODEL),
        out_shape=jax.ShapeDtypeStruct((rows, D_MODEL), F32),
        compiler_params=pltpu.CompilerParams(dimension_semantics=("parallel",),
                                             vmem_limit_bytes=VMEM_LIMIT),
        name="mlp",
    )(x, ohg, oda, w_out, ln_ffn, w_up, w_down, ln_final)


def _rope_tables(pos):
    half = DA_DH // 2
    inv = jnp.power(ROPE_THETA, -jnp.arange(half, dtype=F32) * 2.0 / DA_DH)
    ang = pos.astype(F32)[:, None] * inv[None, :]
    cos, sin = jnp.cos(ang), jnp.sin(ang)
    return jnp.concatenate([cos] * 4, axis=1), jnp.concatenate([-sin, sin, -sin, sin], axis=1)


def _pad_rows(a, n, value=0.0):
    return jnp.pad(a, ((0, 0), (0, n - a.shape[1]), (0, 0)), constant_values=value)


def kernel(x_prompt, x_sample, cache_k, cache_v, state_hgrn, page_table, meta_tokens, ln_mix, w_in,
           hg_lb_logits, hg_norm, da_lambda_q1, da_lambda_k1, da_lambda_q2, da_lambda_k2, da_subln, w_out,
           ln_ffn, w_up, w_down, ln_final):
    nb_p, seq, _ = x_prompt.shape
    nb_s, n_new, _ = x_sample.shape
    depth = w_in.shape[0]
    assert depth == 1, "single-layer step"
    layer = 0
    past_len = page_table.shape[1] * PAGE_SIZE
    lam_init = 0.8 - 0.6 * math.exp(-0.3 * layer)

    w_in_l = w_in[layer].astype(BF16)
    w_out_l = w_out[layer].astype(BF16)
    w_up_l = w_up[layer].astype(BF16)
    w_down_l = w_down[layer].astype(BF16)
    ln_mix_l = ln_mix[layer][None, :]
    ln_ffn_l = ln_ffn[layer][None, :]
    ln_fin = ln_final[None, :]
    hg_norm_l = hg_norm[layer][None, :]
    subln_l = da_subln[layer][None, :]
    lams = [a[layer][None, :] for a in (da_lambda_q1, da_lambda_k1, da_lambda_q2, da_lambda_k2)]

    cos_p, sin_p = _rope_tables(N_META + jnp.arange(seq, dtype=jnp.int32))
    xp = x_prompt.reshape(nb_p * seq, D_MODEL)
    qh_p, f_p, vh_p, gate_p, qr_p, k32_p, k16_p, v32_p, v16_p = _inproj(
        xp, cos_p, sin_p, ln_mix_l, w_in_l, hg_lb_logits, tm=512, layer=layer)

    pos_s = past_len + jnp.arange(n_new, dtype=jnp.int32)
    pos_small = jnp.concatenate([jnp.arange(N_META, dtype=jnp.int32), jnp.tile(pos_s, nb_s)])
    cos_s, sin_s = _rope_tables(pos_small)
    xs = x_sample.reshape(nb_s * n_new, D_MODEL)
    x_small = jnp.concatenate([meta_tokens.astype(F32), xs], axis=0)
    small = _inproj(x_small, cos_s, sin_s, ln_mix_l, w_in_l, hg_lb_logits, tm=x_small.shape[0], layer=layer)
    qh_m, f_m, vh_m, gate_m, _, k32_m, k16_m, v32_m, v16_m = [a[:N_META] for a in small]
    qh_s, f_s, vh_s, gate_s, qr_s, k32_s, k16_s, v32_s, v16_s = [
        a[N_META:].reshape(nb_s, n_new, GROUP_W) for a in small]

    zero_state = jnp.zeros((1, HEADS, HEAD_W, HEAD_W), F32)
    _, s_meta = _hgrn(qh_m, f_m, vh_m, gate_m, zero_state, hg_norm_l, nb=1, chunk=N_META, tm=N_META,
                      out_dtype=F32)
    ohg_p, state_p = _hgrn(qh_p, f_p, vh_p, gate_p, s_meta, hg_norm_l, nb=nb_p, chunk=64, tm=512,
                           out_dtype=BF16)
    cs = 16
    pad = lambda a, val=0.0: _pad_rows(a.astype(F32), cs, val).reshape(nb_s * cs, GROUP_W)
    ohg_s, state_s = _hgrn(pad(qh_s), pad(f_s, 1.0), pad(vh_s), pad(gate_s), state_hgrn[layer], hg_norm_l,
                           nb=nb_s, chunk=cs, tm=cs, out_dtype=F32)
    ohg_s = ohg_s.reshape(nb_s, cs, GROUP_W)[:, :n_new].reshape(nb_s * n_new, GROUP_W)

    k_meta16 = jnp.pad(k16_m, ((0, HEAD_W - N_META), (0, 0)))
    v_meta16 = jnp.pad(v16_m, ((0, HEAD_W - N_META), (0, 0)))
    oda_p = _attn(qr_p, k16_p, v16_p, k_meta16, v_meta16, lams, subln_l, nb=nb_p, seq=seq, tq=256,
                  lam_init=lam_init)

    q4 = qr_s.reshape(nb_s, n_new, HEADS, HEAD_W).transpose(0, 2, 1, 3)
    lane = jnp.arange(HEAD_W) < DA_DH
    q_split = jnp.concatenate([jnp.where(lane, q4, 0), jnp.where(lane, 0, q4)], axis=2).astype(BF16)
    oda_s = _decode_attn(page_table, q_split, _pad_rows(k16_s, PAGE_SIZE), _pad_rows(v16_s, PAGE_SIZE),
                         jnp.transpose(cache_k[layer], (0, 2, 3, 4, 1)).reshape(-1, GROUP_W, PAGE_SIZE),
                         cache_v[layer],
                         lams, subln_l, pages=8, n_new=n_new, lam_init=lam_init)
    oda_s = oda_s[:, :n_new].reshape(nb_s * n_new, GROUP_W)

    y_p = _mlp(xp, ohg_p, oda_p, w_out_l, ln_ffn_l, w_up_l, w_down_l, ln_fin, tm=512)
    y_s = _mlp(xs, ohg_s.astype(BF16), oda_s.astype(BF16), w_out_l, ln_ffn_l, w_up_l, w_down_l, ln_fin,
               tm=nb_s * n_new)

    def with_meta(meta_rows, rows):
        meta_b = jnp.broadcast_to(meta_rows[None], (nb_p, N_META, GROUP_W))
        return jnp.concatenate([meta_b, rows.reshape(nb_p, seq, GROUP_W)], axis=1)

    k_prompt = with_meta(k32_m, k32_p).reshape(1, nb_p, N_META + seq, HEADS, 2, DA_DH)
    v_prompt = with_meta(v32_m, v32_p).reshape(1, nb_p, N_META + seq, HEADS, HEAD_W)
    return (y_p.reshape(nb_p, seq, D_MODEL),
            y_s.reshape(nb_s, n_new, D_MODEL),
            k_prompt,
            v_prompt,
            state_p[None],
            k32_s.reshape(1, nb_s, n_new, HEADS, 2, DA_DH),
            v32_s.reshape(1, nb_s, n_new, HEADS, HEAD_W),
            state_s[None])
```

```python
import functools
import math

import jax
import jax.numpy as jnp
import numpy as np
from jax import lax
from jax.experimental import pallas as pl
from jax.experimental.pallas import tpu as pltpu

F32 = jnp.float32
BF16 = jnp.bfloat16

D_MODEL = 1024
N_META = 16
HEADS = 4
HEAD_W = 128
GROUP_W = HEADS * HEAD_W
DA_DH = 64
PAGE_SIZE = 128
ROPE_THETA = 10000.0
EPS = 1e-6
NEG_INF = -1e30
N_SEG = 7

VMEM_LIMIT = 56 * 1024 * 1024

NT = (((1,), (1,)), ((), ()))
TN = (((0,), (0,)), ((), ()))


def _dot(a, b, dims=None):
    if dims is None:
        return jnp.dot(a, b, preferred_element_type=F32)
    return lax.dot_general(a, b, dims, preferred_element_type=F32)


def _rms(x):
    return x * lax.rsqrt(jnp.mean(x * x, axis=-1, keepdims=True) + EPS)


def _const_spec(shape):
    nd = len(shape)
    return pl.BlockSpec(shape, lambda *_: (0,) * nd, pipeline_mode=pl.Buffered(1))


def _inproj_kernel(x_ref, cos_ref, sin_ref, ln_ref, w_ref, lb_ref,
                   qh_ref, f_ref, vh_ref, gate_ref, qr_ref, k32_ref, k16_ref, v32_ref, v16_ref,
                   *, layer):
    x = x_ref[...]
    h = (_rms(x) * ln_ref[...]).astype(BF16)

    def seg(j):
        return _dot(h, w_ref[:, j * GROUP_W:(j + 1) * GROUP_W])

    lg = lb_ref[...]
    e = jnp.exp(lg - jnp.max(lg, axis=0, keepdims=True))
    sm = e / jnp.sum(e, axis=0, keepdims=True)
    lb = jnp.sum(sm[:layer + 1], axis=0, keepdims=True)

    qh_ref[...] = jax.nn.silu(seg(0)).astype(qh_ref.dtype)
    f_ref[...] = lb + (1.0 - lb) * jax.nn.sigmoid(seg(1))
    vh_ref[...] = seg(2).astype(vh_ref.dtype)
    gate_ref[...] = jax.nn.silu(seg(3)).astype(gate_ref.dtype)

    c4 = jnp.concatenate([cos_ref[...]] * HEADS, axis=1)
    s4 = jnp.concatenate([sin_ref[...]] * HEADS, axis=1)
    lane = lax.broadcasted_iota(jnp.int32, c4.shape, 1)
    first_half = (lane & (DA_DH // 2)) == 0

    def rope(y):
        swapped = jnp.where(first_half,
                            pltpu.roll(y, GROUP_W - DA_DH // 2, 1),
                            pltpu.roll(y, DA_DH // 2, 1))
        return y * c4 + swapped * s4

    qr_ref[...] = (rope(seg(4)) * (DA_DH ** -0.5)).astype(qr_ref.dtype)
    k = rope(seg(5))
    k32_ref[...] = k
    k16_ref[...] = k.astype(BF16)
    v = seg(6)
    v32_ref[...] = v
    v16_ref[...] = v.astype(BF16)


def _inproj(x, cos, sin, ln_mix, w_in, lb_logits, *, tm, layer):
    rows = x.shape[0]
    nper = cos.shape[0] // tm
    grid = (rows // tm,)
    row = lambda w: pl.BlockSpec((tm, w), lambda i: (i, 0))
    tab = pl.BlockSpec((tm, HEAD_W), lambda i: (i % nper, 0))
    outs = [(BF16,), (F32,), (BF16,), (BF16,), (BF16,), (F32,), (BF16,), (F32,), (BF16,)]
    return pl.pallas_call(
        functools.partial(_inproj_kernel, layer=layer),
        grid=grid,
        in_specs=[row(D_MODEL), tab, tab, _const_spec((1, D_MODEL)),
                  _const_spec(w_in.shape), _const_spec(lb_logits.shape)],
        out_specs=[row(GROUP_W)] * len(outs),
        out_shape=[jax.ShapeDtypeStruct((rows, GROUP_W), d[0]) for d in outs],
        compiler_params=pltpu.CompilerParams(dimension_semantics=("parallel",),
                                             vmem_limit_bytes=VMEM_LIMIT),
        name="inproj",
    )(x, cos, sin, ln_mix, w_in, lb_logits)


def _hgrn_weights(chunk):
    levels = int(math.log2(chunk))
    w = np.zeros(((2 + levels) * chunk, chunk), np.float32)
    for t in range(chunk):
        w[t, :t + 1] = 1.0
        w[chunk + t, t + 1:] = 1.0
    row, half = 2 * chunk, chunk // 2
    while half >= 1:
        for t in range(chunk):
            blk = t // half
            if blk % 2 == 1:
                w[row + t, blk * half:t + 1] = 1.0
            else:
                w[row + t, t + 1:blk * half + half] = 1.0
        row += chunk
        half //= 2
    return w


def _hgrn_kernel(q_ref, f_ref, v_ref, gate_ref, s0_ref, w_ref, norm_ref, o_ref, s_out_ref, st_scr,
                 *, chunk, n_chunks):
    i = pl.program_id(1)
    levels = int(math.log2(chunk))

    @pl.when(i == 0)
    def _():
        for h in range(HEADS):
            st_scr[h] = s0_ref[h].T

    row = lax.broadcasted_iota(jnp.int32, (chunk, chunk), 0)
    col = lax.broadcasted_iota(jnp.int32, (chunk, chunk), 1)
    rowv = lax.broadcasted_iota(jnp.int32, (chunk, GROUP_W), 0)
    eye = row == col
    pair_masks, second_half = [], []
    half = chunk // 2
    for _ in range(levels):
        sh = int(math.log2(half))
        pair_masks.append((row >> (sh + 1)) == (col >> (sh + 1)))
        second_half.append(((rowv >> sh) & 1) == 1)
        half //= 2
    w = w_ref[...]
    norm = norm_ref[...]

    def body(c, carry):
        r0 = pl.multiple_of(c * chunk, chunk)
        rows = pl.ds(r0, chunk)
        q = q_ref[rows, :].astype(F32)
        f = f_ref[rows, :]
        v = v_ref[rows, :].astype(BF16)
        gate = gate_ref[rows, :].astype(F32)
        g = jnp.log(f)
        k = 1.0 - f
        g_hi = g.astype(BF16)
        g_lo = (g - g_hi.astype(F32)).astype(BF16)
        ex = _dot(w, jnp.concatenate([g_hi, g_lo], axis=0))
        eb = jnp.exp(ex[0:chunk])
        qe = (q * eb).astype(BF16)
        kk = (k * jnp.exp(ex[chunk:2 * chunk])).astype(BF16)
        qb = q.astype(BF16)
        kb = k.astype(BF16)
        qt, kt = [], []
        zeros = jnp.zeros_like(q)
        for l in range(levels):
            dec = jnp.exp(ex[(2 + l) * chunk:(3 + l) * chunk])
            half = chunk >> (l + 1)
            if half >= 8:
                qp, kp = [], []
                for blk in range(chunk // half):
                    rs = slice(blk * half, (blk + 1) * half)
                    odd = blk % 2 == 1
                    qp.append(q[rs] * dec[rs] if odd else zeros[rs])
                    kp.append(zeros[rs] if odd else k[rs] * dec[rs])
                qt.append(jnp.concatenate(qp, axis=0).astype(BF16))
                kt.append(jnp.concatenate(kp, axis=0).astype(BF16))
            else:
                qt.append(jnp.where(second_half[l], q * dec, 0.0).astype(BF16))
                kt.append(jnp.where(second_half[l], 0.0, k * dec).astype(BF16))
        eb_last = eb[chunk - 1:chunk, :]
        for h in range(HEADS):
            sl = slice(h * HEAD_W, (h + 1) * HEAD_W)
            a = jnp.where(eye, _dot(qb[:, sl], kb[:, sl], NT), 0.0)
            for l in range(levels):
                a = a + jnp.where(pair_masks[l], _dot(qt[l][:, sl], kt[l][:, sl], NT), 0.0)
            st = st_scr[h]
            o = _dot(a.astype(BF16), v[:, sl]) + _dot(qe[:, sl], st.astype(BF16), NT)
            st_scr[h] = st * eb_last[:, sl] + _dot(v[:, sl], kk[:, sl], TN)
            o = _rms(o) * norm * gate[:, sl]
            o_ref[rows, sl] = o.astype(o_ref.dtype)
        return carry

    lax.fori_loop(0, n_chunks, body, 0, unroll=4 if n_chunks % 4 == 0 else 1)

    @pl.when(i == pl.num_programs(1) - 1)
    def _():
        for h in range(HEADS):
            s_out_ref[h] = st_scr[h].T


def _hgrn(q, f, v, gate, s0, hg_norm, *, nb, chunk, tm, out_dtype):
    rows = q.shape[0]
    nblk = rows // nb // tm
    w = _hgrn_weights(chunk)
    w = jnp.asarray(np.concatenate([w, w], axis=1), BF16)
    per_batch_state = s0.shape[0] == nb
    row = pl.BlockSpec((tm, GROUP_W), lambda b, i: (b * nblk + i, 0))
    st_in = pl.BlockSpec((None, HEADS, HEAD_W, HEAD_W),
                         (lambda b, i: (b, 0, 0, 0)) if per_batch_state else (lambda b, i: (0, 0, 0, 0)))
    st_out = pl.BlockSpec((None, HEADS, HEAD_W, HEAD_W), lambda b, i: (b, 0, 0, 0))
    return pl.pallas_call(
        functools.partial(_hgrn_kernel, chunk=chunk, n_chunks=tm // chunk),
        grid=(nb, nblk),
        in_specs=[row, row, row, row, st_in, _const_spec(w.shape), _const_spec((1, HEAD_W))],
        out_specs=[row, st_out],
        out_shape=[jax.ShapeDtypeStruct((rows, GROUP_W), out_dtype),
                   jax.ShapeDtypeStruct((nb, HEADS, HEAD_W, HEAD_W), F32)],
        scratch_shapes=[pltpu.VMEM((HEADS, HEAD_W, HEAD_W), F32)],
        compiler_params=pltpu.CompilerParams(dimension_semantics=("parallel", "arbitrary"),
                                             vmem_limit_bytes=VMEM_LIMIT),
        name=f"hgrn_c{chunk}",
    )(q, f, v, gate, s0, w, hg_norm)


def _lambda(lq1, lk1, lq2, lk2, lam_init):
    return (jnp.exp(jnp.sum(lq1 * lk1, axis=-1, keepdims=True))
            - jnp.exp(jnp.sum(lq2 * lk2, axis=-1, keepdims=True)) + lam_init)


def _softmax_step(s, v, m_ref, acc_ref):
    blocks = [s[:, j:j + HEAD_W] for j in range(0, s.shape[1], HEAD_W)]
    mx = functools.reduce(jnp.maximum, blocks)
    m_prev = m_ref[...]
    m_new = jnp.maximum(m_prev, jnp.max(mx, axis=-1, keepdims=True))
    alpha = jnp.exp(m_prev - m_new)
    p = jnp.concatenate([jnp.exp(b - m_new).astype(BF16) for b in blocks], axis=1)
    v_aug = jnp.concatenate([v, jnp.ones_like(v)], axis=1)
    acc_ref[...] = jnp.concatenate([alpha, alpha], axis=1) * acc_ref[...] + _dot(p, v_aug)
    m_ref[...] = m_new


def _softmax_init(m_ref, acc_ref):
    m_ref[...] = jnp.full(m_ref.shape, NEG_INF, F32)
    acc_ref[...] = jnp.zeros(acc_ref.shape, F32)


def _softmax_result(acc):
    return acc[:, :HEAD_W] / acc[:, HEAD_W:]


def _split_maps(q):
    lane = lax.broadcasted_iota(jnp.int32, q.shape, 1)
    zero = jnp.zeros_like(q)
    return jnp.concatenate([jnp.where(lane < DA_DH, q, zero), jnp.where(lane >= DA_DH, q, zero)], axis=0)


def _attn_kernel(q_ref, k_ref, v_ref, km_ref, vm_ref, lq1_ref, lk1_ref, lq2_ref, lk2_ref, subln_ref,
                 o_ref, m_scr, acc_scr, *, tq, lam_init):
    qi = pl.program_id(2)
    qbd = _split_maps(q_ref[...])
    col = lax.broadcasted_iota(jnp.int32, (2 * tq, HEAD_W), 1)

    _softmax_init(m_scr, acc_scr)
    s = _dot(qbd, km_ref[...], NT)
    _softmax_step(jnp.where(col < N_META, s, NEG_INF), vm_ref[...], m_scr, acc_scr)

    def block(j):
        return pl.ds(pl.multiple_of(j * tq, tq), tq)

    def scores(j):
        return _dot(qbd, k_ref[block(j), :], NT)

    def body(j, s):
        s_next = scores(j + 1)
        _softmax_step(s, v_ref[block(j), :], m_scr, acc_scr)
        return s_next

    s = lax.fori_loop(0, qi, body, scores(0))
    r = lax.broadcasted_iota(jnp.int32, s.shape, 0)
    c = lax.broadcasted_iota(jnp.int32, s.shape, 1)
    r = jnp.where(r >= tq, r - tq, r)
    _softmax_step(jnp.where(c <= r, s, NEG_INF), v_ref[block(qi), :], m_scr, acc_scr)

    lam = _lambda(lq1_ref[...], lk1_ref[...], lq2_ref[...], lk2_ref[...], lam_init)
    out = _softmax_result(acc_scr[...])
    o = out[:tq] - lam * out[tq:]
    o = _rms(o) * subln_ref[...] * (1.0 - lam_init)
    o_ref[...] = o.astype(o_ref.dtype)


def _attn(q, k, v, k_meta, v_meta, lams, subln, *, nb, seq, tq, lam_init):
    nq = seq // tq
    qspec = pl.BlockSpec((tq, HEAD_W), lambda b, h, i: (b * nq + i, h))
    kvspec = pl.BlockSpec((seq, HEAD_W), lambda b, h, i: (b, h))
    mspec = pl.BlockSpec((HEAD_W, HEAD_W), lambda b, h, i: (0, h))
    vec = lambda n: pl.BlockSpec((1, n), lambda b, h, i: (0, 0))
    return pl.pallas_call(
        functools.partial(_attn_kernel, tq=tq, lam_init=lam_init),
        grid=(nb, HEADS, nq),
        in_specs=[qspec, kvspec, kvspec, mspec, mspec, vec(DA_DH), vec(DA_DH), vec(DA_DH), vec(DA_DH),
                  vec(HEAD_W)],
        out_specs=qspec,
        out_shape=jax.ShapeDtypeStruct((nb * seq, GROUP_W), BF16),
        scratch_shapes=[pltpu.VMEM((2 * tq, HEAD_W), F32), pltpu.VMEM((2 * tq, 2 * HEAD_W), F32)],
        compiler_params=pltpu.CompilerParams(dimension_semantics=("parallel", "parallel", "arbitrary"),
                                             vmem_limit_bytes=VMEM_LIMIT),
        name="prompt_attn",
    )(q, k, v, k_meta, v_meta, *lams, subln)


def _decode_kernel(pt_ref, q_ref, kn_ref, vn_ref, lq1_ref, lk1_ref, lq2_ref, lk2_ref, subln_ref, *refs,
                   pages, n_new, lam_init):
    del pt_ref
    k_pages, v_pages = refs[:pages], refs[pages:2 * pages]
    o_ref, m_scr, acc_scr = refs[2 * pages:]
    s_idx = pl.program_id(1)
    rows = 2 * n_new

    @pl.when(s_idx == 0)
    def _():
        _softmax_init(m_scr, acc_scr)

    for h in range(HEADS):
        sl = slice(h * HEAD_W, (h + 1) * HEAD_W)
        kh = jnp.concatenate([r[sl, :] for r in k_pages], axis=1).astype(BF16)
        vh = jnp.concatenate([r[pl.ds(h, PAGE_SIZE, stride=HEADS), :] for r in v_pages], axis=0).astype(BF16)
        _softmax_step(_dot(q_ref[h], kh), vh, m_scr.at[h], acc_scr.at[h])

    @pl.when(s_idx == pl.num_programs(1) - 1)
    def _():
        lam = _lambda(lq1_ref[...], lk1_ref[...], lq2_ref[...], lk2_ref[...], lam_init)
        r = lax.broadcasted_iota(jnp.int32, (rows, HEAD_W), 0)
        c = lax.broadcasted_iota(jnp.int32, (rows, HEAD_W), 1)
        r = jnp.where(r >= n_new, r - n_new, r)
        visible = c <= r
        for h in range(HEADS):
            sl = slice(h * HEAD_W, (h + 1) * HEAD_W)
            s = _dot(q_ref[h], kn_ref[:, sl], NT)
            _softmax_step(jnp.where(visible, s, NEG_INF), vn_ref[:, sl], m_scr.at[h], acc_scr.at[h])
            out = _softmax_result(acc_scr[h])
            o = out - lam * pltpu.roll(out, rows - n_new, 0)
            o = _rms(o) * subln_ref[...] * (1.0 - lam_init)
            o_ref[:, sl] = o


def _decode_attn(page_table, q_split, k_new, v_new, cache_k, cache_v, lams, subln, *, pages, n_new, lam_init):
    nb, n_pages = page_table.shape
    steps = n_pages // pages
    rows = 2 * n_new
    per_b = lambda shape: pl.BlockSpec((None,) + shape, lambda b, s, pt: (b,) + (0,) * len(shape))
    vec = lambda n: pl.BlockSpec((1, n), lambda b, s, pt: (0, 0))

    def k_spec(p):
        return pl.BlockSpec((None, GROUP_W, PAGE_SIZE), lambda b, s, pt: (pt[b, s * pages + p], 0, 0))

    def v_spec(p):
        return pl.BlockSpec((None, PAGE_SIZE * HEADS, HEAD_W), lambda b, s, pt: (pt[b, s * pages + p], 0, 0))

    grid_spec = pltpu.PrefetchScalarGridSpec(
        num_scalar_prefetch=1,
        grid=(nb, steps),
        in_specs=[per_b((HEADS, rows, HEAD_W)), per_b((PAGE_SIZE, GROUP_W)), per_b((PAGE_SIZE, GROUP_W)),
                  vec(DA_DH), vec(DA_DH), vec(DA_DH), vec(DA_DH), vec(HEAD_W)]
                 + [k_spec(p) for p in range(pages)] + [v_spec(p) for p in range(pages)],
        out_specs=per_b((rows, GROUP_W)),
        scratch_shapes=[pltpu.VMEM((HEADS, rows, HEAD_W), F32), pltpu.VMEM((HEADS, rows, 2 * HEAD_W), F32)],
    )
    return pl.pallas_call(
        functools.partial(_decode_kernel, pages=pages, n_new=n_new, lam_init=lam_init),
        grid_spec=grid_spec,
        out_shape=jax.ShapeDtypeStruct((nb, rows, GROUP_W), F32),
        compiler_params=pltpu.CompilerParams(dimension_semantics=("parallel", "arbitrary"),
                                             vmem_limit_bytes=VMEM_LIMIT),
        name="decode_attn",
    )(page_table, q_split, k_new, v_new, *lams, subln, *([cache_k] * pages), *([cache_v] * pages))


def _mlp_kernel(x_ref, ohg_ref, oda_ref, wo_ref, lnf_ref, wu_ref, wd_ref, lnfin_ref, y_ref, *, ff_chunk):
    x = x_ref[...]
    x = x + _dot(ohg_ref[...], wo_ref[0:GROUP_W, :]) + _dot(oda_ref[...], wo_ref[GROUP_W:2 * GROUP_W, :])
    hn = (_rms(x) * lnf_ref[...]).astype(BF16)
    acc = jnp.zeros_like(x)
    for c in range(wu_ref.shape[1] // ff_chunk):
        cs = slice(c * ff_chunk, (c + 1) * ff_chunk)
        u = jnp.maximum(_dot(hn, wu_ref[:, cs]), 0.0)
        acc = acc + _dot((u * u).astype(BF16), wd_ref[cs, :])
    y_ref[...] = _rms(x + acc) * lnfin_ref[...]


def _mlp(x, ohg, oda, w_out, ln_ffn, w_up, w_down, ln_final, *, tm, ff_chunk=512):
    rows = x.shape[0]
    row = lambda w: pl.BlockSpec((tm, w), lambda i: (i, 0))
    return pl.pallas_call(
        functools.partial(_mlp_kernel, ff_chunk=ff_chunk),
        grid=(rows // tm,),
        in_specs=[row(D_MODEL), row(GROUP_W), row(GROUP_W), _const_spec(w_out.shape), _const_spec((1, D_MODEL)),
                  _const_spec(w_up.shape), _const_spec(w_down.shape), _const_spec((1, D_MODEL))],
        out_specs=row(D_MODEL),
        out_shape=jax.ShapeDtypeStruct((rows, D_MODEL), F32),
        compiler_params=pltpu.CompilerParams(dimension_semantics=("parallel",),
                                             vmem_limit_bytes=VMEM_LIMIT),
        name="mlp",
    )(x, ohg, oda, w_out, ln_ffn, w_up, w_down, ln_final)


def _rope_tables(pos):
    half = DA_DH // 2
    inv = jnp.power(ROPE_THETA, -jnp.arange(half, dtype=F32) * 2.0 / DA_DH)
    ang = pos.astype(F32)[:, None] * inv[None, :]
    cos, sin = jnp.cos(ang), jnp.sin(ang)
    return jnp.concatenate([cos] * 4, axis=1), jnp.concatenate([-sin, sin, -sin, sin], axis=1)


def _pad_rows(a, n, value=0.0):
    return jnp.pad(a, ((0, 0), (0, n - a.shape[1]), (0, 0)), constant_values=value)


def kernel(x_prompt, x_sample, cache_k, cache_v, state_hgrn, page_table, meta_tokens, ln_mix, w_in,
           hg_lb_logits, hg_norm, da_lambda_q1, da_lambda_k1, da_lambda_q2, da_lambda_k2, da_subln, w_out,
           ln_ffn, w_up, w_down, ln_final):
    nb_p, seq, _ = x_prompt.shape
    nb_s, n_new, _ = x_sample.shape
    depth = w_in.shape[0]
    assert depth == 1, "single-layer step"
    layer = 0
    past_len = page_table.shape[1] * PAGE_SIZE
    lam_init = 0.8 - 0.6 * math.exp(-0.3 * layer)

    w_in_l = w_in[layer].astype(BF16)
    w_out_l = w_out[layer].astype(BF16)
    w_up_l = w_up[layer].astype(BF16)
    w_down_l = w_down[layer].astype(BF16)
    ln_mix_l = ln_mix[layer][None, :]
    ln_ffn_l = ln_ffn[layer][None, :]
    ln_fin = ln_final[None, :]
    hg_norm_l = hg_norm[layer][None, :]
    subln_l = da_subln[layer][None, :]
    lams = [a[layer][None, :] for a in (da_lambda_q1, da_lambda_k1, da_lambda_q2, da_lambda_k2)]

    cos_p, sin_p = _rope_tables(N_META + jnp.arange(seq, dtype=jnp.int32))
    xp = x_prompt.reshape(nb_p * seq, D_MODEL)
    qh_p, f_p, vh_p, gate_p, qr_p, k32_p, k16_p, v32_p, v16_p = _inproj(
        xp, cos_p, sin_p, ln_mix_l, w_in_l, hg_lb_logits, tm=512, layer=layer)

    pos_s = past_len + jnp.arange(n_new, dtype=jnp.int32)
    pos_small = jnp.concatenate([jnp.arange(N_META, dtype=jnp.int32), jnp.tile(pos_s, nb_s)])
    cos_s, sin_s = _rope_tables(pos_small)
    xs = x_sample.reshape(nb_s * n_new, D_MODEL)
    x_small = jnp.concatenate([meta_tokens.astype(F32), xs], axis=0)
    small = _inproj(x_small, cos_s, sin_s, ln_mix_l, w_in_l, hg_lb_logits, tm=x_small.shape[0], layer=layer)
    qh_m, f_m, vh_m, gate_m, _, k32_m, k16_m, v32_m, v16_m = [a[:N_META] for a in small]
    qh_s, f_s, vh_s, gate_s, qr_s, k32_s, k16_s, v32_s, v16_s = [
        a[N_META:].reshape(nb_s, n_new, GROUP_W) for a in small]

    zero_state = jnp.zeros((1, HEADS, HEAD_W, HEAD_W), F32)
    _, s_meta = _hgrn(qh_m, f_m, vh_m, gate_m, zero_state, hg_norm_l, nb=1, chunk=N_META, tm=N_META,
                      out_dtype=F32)
    ohg_p, state_p = _hgrn(qh_p, f_p, vh_p, gate_p, s_meta, hg_norm_l, nb=nb_p, chunk=64, tm=512,
                           out_dtype=BF16)
    cs = 16
    pad = lambda a, val=0.0: _pad_rows(a.astype(F32), cs, val).reshape(nb_s * cs, GROUP_W)
    ohg_s, state_s = _hgrn(pad(qh_s), pad(f_s, 1.0), pad(vh_s), pad(gate_s), state_hgrn[layer], hg_norm_l,
                           nb=nb_s, chunk=cs, tm=cs, out_dtype=F32)
    ohg_s = ohg_s.reshape(nb_s, cs, GROUP_W)[:, :n_new].reshape(nb_s * n_new, GROUP_W)

    k_meta16 = jnp.pad(k16_m, ((0, HEAD_W - N_META), (0, 0)))
    v_meta16 = jnp.pad(v16_m, ((0, HEAD_W - N_META), (0, 0)))
    oda_p = _attn(qr_p, k16_p, v16_p, k_meta16, v_meta16, lams, subln_l, nb=nb_p, seq=seq, tq=512,
                  lam_init=lam_init)

    q4 = qr_s.reshape(nb_s, n_new, HEADS, HEAD_W).transpose(0, 2, 1, 3)
    lane = jnp.arange(HEAD_W) < DA_DH
    q_split = jnp.concatenate([jnp.where(lane, q4, 0), jnp.where(lane, 0, q4)], axis=2).astype(BF16)
    oda_s = _decode_attn(page_table, q_split, _pad_rows(k16_s, PAGE_SIZE), _pad_rows(v16_s, PAGE_SIZE),
                         jnp.transpose(cache_k[layer], (0, 2, 3, 4, 1)).reshape(-1, GROUP_W, PAGE_SIZE),
                         cache_v[layer].reshape(-1, PAGE_SIZE * HEADS, HEAD_W),
                         lams, subln_l, pages=8, n_new=n_new, lam_init=lam_init)
    oda_s = oda_s[:, :n_new].reshape(nb_s * n_new, GROUP_W)

    y_p = _mlp(xp, ohg_p, oda_p, w_out_l, ln_ffn_l, w_up_l, w_down_l, ln_fin, tm=512)
    y_s = _mlp(xs, ohg_s.astype(BF16), oda_s.astype(BF16), w_out_l, ln_ffn_l, w_up_l, w_down_l, ln_fin,
               tm=nb_s * n_new)

    def with_meta(meta_rows, rows):
        meta_b = jnp.broadcast_to(meta_rows[None], (nb_p, N_META, GROUP_W))
        return jnp.concatenate([meta_b, rows.reshape(nb_p, seq, GROUP_W)], axis=1)

    k_prompt = with_meta(k32_m, k32_p).reshape(1, nb_p, N_META + seq, HEADS, 2, DA_DH)
    v_prompt = with_meta(v32_m, v32_p).reshape(1, nb_p, N_META + seq, HEADS, HEAD_W)
    return (y_p.reshape(nb_p, seq, D_MODEL),
            y_s.reshape(nb_s, n_new, D_MODEL),
            k_prompt,
            v_prompt,
            state_p[None],
            k32_s.reshape(1, nb_s, n_new, HEADS, 2, DA_DH),
            v32_s.reshape(1, nb_s, n_new, HEADS, HEAD_W),
            state_s[None])
```

```python
import functools
import math

import jax
import jax.numpy as jnp
import numpy as np
from jax import lax
from jax.experimental import pallas as pl
from jax.experimental.pallas import tpu as pltpu

F32 = jnp.float32
BF16 = jnp.bfloat16

D_MODEL = 1024
N_META = 16
HEADS = 4
HEAD_W = 128
GROUP_W = HEADS * HEAD_W
DA_DH = 64
PAGE_SIZE = 128
ROPE_THETA = 10000.0
EPS = 1e-6
NEG_INF = -1e30
N_SEG = 7

VMEM_LIMIT = 56 * 1024 * 1024

NT = (((1,), (1,)), ((), ()))
TN = (((0,), (0,)), ((), ()))


def _dot(a, b, dims=None):
    if dims is None:
        return jnp.dot(a, b, preferred_element_type=F32)
    return lax.dot_general(a, b, dims, preferred_element_type=F32)


def _rms(x):
    return x * lax.rsqrt(jnp.mean(x * x, axis=-1, keepdims=True) + EPS)


def _const_spec(shape):
    nd = len(shape)
    return pl.BlockSpec(shape, lambda *_: (0,) * nd, pipeline_mode=pl.Buffered(1))


def _inproj_rows(x_ref, cos_ref, sin_ref, ln_ref, w_ref, lb_ref,
                 qh_ref, f_ref, vh_ref, gate_ref, qr_ref, k16_ref, v16_ref, *, layer):
    x = x_ref[...]
    h = (_rms(x) * ln_ref[...]).astype(BF16)

    def seg(j):
        return _dot(h, w_ref[:, j * GROUP_W:(j + 1) * GROUP_W])

    lg = lb_ref[...]
    e = jnp.exp(lg - jnp.max(lg, axis=0, keepdims=True))
    sm = e / jnp.sum(e, axis=0, keepdims=True)
    lb = jnp.sum(sm[:layer + 1], axis=0, keepdims=True)

    qh_ref[...] = jax.nn.silu(seg(0)).astype(qh_ref.dtype)
    f_ref[...] = lb + (1.0 - lb) * jax.nn.sigmoid(seg(1))
    vh_ref[...] = seg(2).astype(vh_ref.dtype)
    gate_ref[...] = jax.nn.silu(seg(3)).astype(gate_ref.dtype)

    c4 = jnp.concatenate([cos_ref[...]] * HEADS, axis=1)
    s4 = jnp.concatenate([sin_ref[...]] * HEADS, axis=1)
    lane = lax.broadcasted_iota(jnp.int32, c4.shape, 1)
    first_half = (lane & (DA_DH // 2)) == 0

    def rope(y):
        swapped = jnp.where(first_half,
                            pltpu.roll(y, GROUP_W - DA_DH // 2, 1),
                            pltpu.roll(y, DA_DH // 2, 1))
        return y * c4 + swapped * s4

    qr_ref[...] = (rope(seg(4)) * (DA_DH ** -0.5 * math.log2(math.e))).astype(qr_ref.dtype)
    k = rope(seg(5))
    k16_ref[...] = k.astype(BF16)
    v = seg(6)
    v16_ref[...] = v.astype(BF16)
    return k, v


def _inproj_kernel(*refs, layer):
    k32_ref, v32_ref = refs[-2:]
    k32_ref[...], v32_ref[...] = _inproj_rows(*refs[:-2], layer=layer)


def _inproj_prompt_kernel(x_ref, cos_ref, sin_ref, ln_ref, w_ref, lb_ref, kmeta_ref, vmeta_ref,
                          qh_ref, f_ref, vh_ref, gate_ref, qr_ref, k16_ref, v16_ref, kfull_ref, vfull_ref,
                          kbuf, vbuf, sem, meta_sem, *, layer, tm, nblk, seq):
    g = pl.program_id(0)
    last = pl.num_programs(0) - 1
    per_prompt = N_META + seq

    def copies(step):
        row0 = (step // nblk) * per_prompt + N_META + (step % nblk) * tm
        s = step % 2
        return (pltpu.make_async_copy(kbuf.at[s], kfull_ref.at[pl.ds(row0, tm), :], sem.at[0, s]),
                pltpu.make_async_copy(vbuf.at[s], vfull_ref.at[pl.ds(row0 * HEADS, tm * HEADS), :], sem.at[1, s]))

    @pl.when(g >= 2)
    def _():
        for c in copies(g - 2):
            c.wait()

    k, v = _inproj_rows(x_ref, cos_ref, sin_ref, ln_ref, w_ref, lb_ref,
                        qh_ref, f_ref, vh_ref, gate_ref, qr_ref, k16_ref, v16_ref, layer=layer)
    slot = g % 2
    kbuf[slot] = k
    for h in range(HEADS):
        vbuf.at[slot][pl.ds(h, tm, stride=HEADS), :] = v[:, h * HEAD_W:(h + 1) * HEAD_W]
    for c in copies(g):
        c.start()

    @pl.when(g % nblk == 0)
    def _():
        base = (g // nblk) * per_prompt
        mk = pltpu.make_async_copy(kmeta_ref, kfull_ref.at[pl.ds(base, N_META), :], meta_sem.at[0])
        mv = pltpu.make_async_copy(vmeta_ref, vfull_ref.at[pl.ds(base * HEADS, N_META * HEADS), :], meta_sem.at[1])
        mk.start()
        mv.start()
        mk.wait()
        mv.wait()

    @pl.when(g == last)
    def _():
        for c in copies(g) + copies(g - 1):
            c.wait()


def _inproj(x, cos, sin, ln_mix, w_in, lb_logits, *, tm, layer, meta_kv=None, seq=None):
    rows = x.shape[0]
    nper = cos.shape[0] // tm
    grid = (rows // tm,)
    row = lambda w: pl.BlockSpec((tm, w), lambda i: (i, 0))
    tab = pl.BlockSpec((tm, HEAD_W), lambda i: (i % nper, 0))
    in_specs = [row(D_MODEL), tab, tab, _const_spec((1, D_MODEL)), _const_spec(w_in.shape),
                _const_spec(lb_logits.shape)]
    narrow = [BF16, F32, BF16, BF16, BF16, BF16, BF16]
    out_specs = [row(GROUP_W)] * len(narrow)
    out_shape = [jax.ShapeDtypeStruct((rows, GROUP_W), d) for d in narrow]
    if meta_kv is None:
        return pl.pallas_call(
            functools.partial(_inproj_kernel, layer=layer),
            grid=grid, in_specs=in_specs,
            out_specs=out_specs + [row(GROUP_W)] * 2,
            out_shape=out_shape + [jax.ShapeDtypeStruct((rows, GROUP_W), F32)] * 2,
            compiler_params=pltpu.CompilerParams(dimension_semantics=("parallel",),
                                                 vmem_limit_bytes=VMEM_LIMIT),
            name="inproj",
        )(x, cos, sin, ln_mix, w_in, lb_logits)
    nblk = seq // tm
    full_rows = (rows // seq) * (N_META + seq)
    any_spec = pl.BlockSpec(memory_space=pl.ANY)
    return pl.pallas_call(
        functools.partial(_inproj_prompt_kernel, layer=layer, tm=tm, nblk=nblk, seq=seq),
        grid=grid,
        in_specs=in_specs + [_const_spec(meta_kv[0].shape), _const_spec(meta_kv[1].shape)],
        out_specs=out_specs + [any_spec, any_spec],
        out_shape=out_shape + [jax.ShapeDtypeStruct((full_rows, GROUP_W), F32),
                               jax.ShapeDtypeStruct((full_rows * HEADS, HEAD_W), F32)],
        scratch_shapes=[pltpu.VMEM((2, tm, GROUP_W), F32), pltpu.VMEM((2, tm * HEADS, HEAD_W), F32),
                        pltpu.SemaphoreType.DMA((2, 2)), pltpu.SemaphoreType.DMA((2,))],
        compiler_params=pltpu.CompilerParams(dimension_semantics=("arbitrary",),
                                             vmem_limit_bytes=VMEM_LIMIT),
        name="inproj_prompt",
    )(x, cos, sin, ln_mix, w_in, lb_logits, *meta_kv)


def _hgrn_weights(chunk):
    levels = int(math.log2(chunk))
    w = np.zeros(((2 + levels) * chunk, chunk), np.float32)
    for t in range(chunk):
        w[t, :t + 1] = 1.0
        w[chunk + t, t + 1:] = 1.0
    row, half = 2 * chunk, chunk // 2
    while half >= 1:
        for t in range(chunk):
            blk = t // half
            if blk % 2 == 1:
                w[row + t, blk * half:t + 1] = 1.0
            else:
                w[row + t, t + 1:blk * half + half] = 1.0
        row += chunk
        half //= 2
    return w


def _hgrn_kernel(q_ref, f_ref, v_ref, gate_ref, s0_ref, w_ref, norm_ref, o_ref, s_out_ref, st_scr,
                 *, chunk, n_chunks):
    i = pl.program_id(1)
    levels = int(math.log2(chunk))

    @pl.when(i == 0)
    def _():
        for h in range(HEADS):
            st_scr[h] = s0_ref[h].T

    row = lax.broadcasted_iota(jnp.int32, (chunk, chunk), 0)
    col = lax.broadcasted_iota(jnp.int32, (chunk, chunk), 1)
    rowv = lax.broadcasted_iota(jnp.int32, (chunk, GROUP_W), 0)
    eye = row == col
    pair_masks, second_half = [], []
    half = chunk // 2
    for _ in range(levels):
        sh = int(math.log2(half))
        pair_masks.append((row >> (sh + 1)) == (col >> (sh + 1)))
        second_half.append(((rowv >> sh) & 1) == 1)
        half //= 2
    w = w_ref[...]
    norm = norm_ref[...]

    def body(c, carry):
        r0 = pl.multiple_of(c * chunk, chunk)
        rows = pl.ds(r0, chunk)
        q = q_ref[rows, :].astype(F32)
        f = f_ref[rows, :]
        v = v_ref[rows, :].astype(BF16)
        gate = gate_ref[rows, :].astype(F32)
        g = jnp.log(f)
        k = 1.0 - f
        g_hi = g.astype(BF16)
        g_lo = (g - g_hi.astype(F32)).astype(BF16)
        ex = _dot(w, jnp.concatenate([g_hi, g_lo], axis=0))
        eb = jnp.exp(ex[0:chunk])
        qe = (q * eb).astype(BF16)
        kk = (k * jnp.exp(ex[chunk:2 * chunk])).astype(BF16)
        qb = q.astype(BF16)
        kb = k.astype(BF16)
        qt, kt = [], []
        zeros = jnp.zeros_like(q)
        for l in range(levels):
            dec = jnp.exp(ex[(2 + l) * chunk:(3 + l) * chunk])
            half = chunk >> (l + 1)
            if half >= 8:
                qp, kp = [], []
                for blk in range(chunk // half):
                    rs = slice(blk * half, (blk + 1) * half)
                    odd = blk % 2 == 1
                    qp.append(q[rs] * dec[rs] if odd else zeros[rs])
                    kp.append(zeros[rs] if odd else k[rs] * dec[rs])
                qt.append(jnp.concatenate(qp, axis=0).astype(BF16))
                kt.append(jnp.concatenate(kp, axis=0).astype(BF16))
            else:
                qt.append(jnp.where(second_half[l], q * dec, 0.0).astype(BF16))
                kt.append(jnp.where(second_half[l], 0.0, k * dec).astype(BF16))
        eb_last = eb[chunk - 1:chunk, :]
        for h in range(HEADS):
            sl = slice(h * HEAD_W, (h + 1) * HEAD_W)
            a = jnp.where(eye, _dot(qb[:, sl], kb[:, sl], NT), 0.0)
            for l in range(levels):
                a = a + jnp.where(pair_masks[l], _dot(qt[l][:, sl], kt[l][:, sl], NT), 0.0)
            st = st_scr[h]
            o = _dot(a.astype(BF16), v[:, sl]) + _dot(qe[:, sl], st.astype(BF16), NT)
            st_scr[h] = st * eb_last[:, sl] + _dot(v[:, sl], kk[:, sl], TN)
            o = _rms(o) * norm * gate[:, sl]
            o_ref[rows, sl] = o.astype(o_ref.dtype)
        return carry

    lax.fori_loop(0, n_chunks, body, 0, unroll=4 if n_chunks % 4 == 0 else 1)

    @pl.when(i == pl.num_programs(1) - 1)
    def _():
        for h in range(HEADS):
            s_out_ref[h] = st_scr[h].T


def _hgrn(q, f, v, gate, s0, hg_norm, *, nb, chunk, tm, out_dtype):
    rows = q.shape[0]
    nblk = rows // nb // tm
    w = _hgrn_weights(chunk)
    w = jnp.asarray(np.concatenate([w, w], axis=1), BF16)
    per_batch_state = s0.shape[0] == nb
    row = pl.BlockSpec((tm, GROUP_W), lambda b, i: (b * nblk + i, 0))
    st_in = pl.BlockSpec((None, HEADS, HEAD_W, HEAD_W),
                         (lambda b, i: (b, 0, 0, 0)) if per_batch_state else (lambda b, i: (0, 0, 0, 0)))
    st_out = pl.BlockSpec((None, HEADS, HEAD_W, HEAD_W), lambda b, i: (b, 0, 0, 0))
    return pl.pallas_call(
        functools.partial(_hgrn_kernel, chunk=chunk, n_chunks=tm // chunk),
        grid=(nb, nblk),
        in_specs=[row, row, row, row, st_in, _const_spec(w.shape), _const_spec((1, HEAD_W))],
        out_specs=[row, st_out],
        out_shape=[jax.ShapeDtypeStruct((rows, GROUP_W), out_dtype),
                   jax.ShapeDtypeStruct((nb, HEADS, HEAD_W, HEAD_W), F32)],
        scratch_shapes=[pltpu.VMEM((HEADS, HEAD_W, HEAD_W), F32)],
        compiler_params=pltpu.CompilerParams(dimension_semantics=("parallel", "arbitrary"),
                                             vmem_limit_bytes=VMEM_LIMIT),
        name=f"hgrn_c{chunk}",
    )(q, f, v, gate, s0, w, hg_norm)


def _lambda(lq1, lk1, lq2, lk2, lam_init):
    return (jnp.exp(jnp.sum(lq1 * lk1, axis=-1, keepdims=True))
            - jnp.exp(jnp.sum(lq2 * lk2, axis=-1, keepdims=True)) + lam_init)


def _softmax_step(s, v, m_ref, acc_ref):
    alpha, p = _softmax_probs(s, m_ref)
    acc_ref[...] = _both_halves(alpha) * acc_ref[...] + _pv(p, v)


def _softmax_probs(s, m_ref):
    blocks = [s[:, j:j + HEAD_W] for j in range(0, s.shape[1], HEAD_W)]
    mx = functools.reduce(jnp.maximum, blocks)
    m_prev = m_ref[...]
    m_new = jnp.maximum(m_prev, jnp.max(mx, axis=-1, keepdims=True))
    m_ref[...] = m_new
    return jnp.exp2(m_prev - m_new), jnp.concatenate([jnp.exp2(b - m_new).astype(BF16) for b in blocks], axis=1)


def _pv(p, v):
    return _dot(p, jnp.concatenate([v, jnp.ones_like(v)], axis=1))


def _both_halves(alpha):
    return jnp.concatenate([alpha, alpha], axis=1)


def _softmax_init(m_ref, acc_ref):
    m_ref[...] = jnp.full(m_ref.shape, NEG_INF, F32)
    acc_ref[...] = jnp.zeros(acc_ref.shape, F32)


def _softmax_result(acc):
    return acc[:, :HEAD_W] / acc[:, HEAD_W:]


def _split_maps(q):
    lane = lax.broadcasted_iota(jnp.int32, q.shape, 1)
    zero = jnp.zeros_like(q)
    return jnp.concatenate([jnp.where(lane < DA_DH, q, zero), jnp.where(lane >= DA_DH, q, zero)], axis=0)


def _attn_kernel(q_ref, k_ref, v_ref, km_ref, vm_ref, lq1_ref, lk1_ref, lq2_ref, lk2_ref, subln_ref,
                 o_ref, m_scr, acc_scr, *, tq, lam_init):
    qi = pl.program_id(2)
    qbd = _split_maps(q_ref[...])
    col = lax.broadcasted_iota(jnp.int32, (2 * tq, HEAD_W), 1)

    _softmax_init(m_scr, acc_scr)

    def block(j):
        return pl.ds(pl.multiple_of(j * tq, tq), tq)

    def scores(j):
        return _dot(qbd, k_ref[block(j), :], NT)

    def values(j):
        return v_ref[block(j), :]

    def causal(s):
        r = lax.broadcasted_iota(jnp.int32, s.shape, 0)
        c = lax.broadcasted_iota(jnp.int32, s.shape, 1)
        return jnp.where(c <= jnp.where(r >= tq, r - tq, r), s, NEG_INF)

    def body(j, s):
        s_next = scores(j + 1)
        _softmax_step(s, values(j), m_scr, acc_scr)
        return s_next

    s = lax.fori_loop(0, qi, body, scores(0))
    _softmax_step(causal(s), values(qi), m_scr, acc_scr)

    s = _dot(qbd, km_ref[...], NT)
    _softmax_step(jnp.where(col < N_META, s, NEG_INF), vm_ref[...], m_scr, acc_scr)

    lam =_lambda(lq1_ref[...], lk1_ref[...], lq2_ref[...], lk2_ref[...], lam_init)
    out = _softmax_result(acc_scr[...])
    o = out[:tq] - lam * out[tq:]
    o = _rms(o) * subln_ref[...] * (1.0 - lam_init)
    o_ref[...] = o.astype(o_ref.dtype)


def _attn(q, k, v, k_meta, v_meta, lams, subln, *, nb, seq, tq, lam_init):
    nq = seq // tq
    qspec = pl.BlockSpec((tq, HEAD_W), lambda b, h, i: (b * nq + i, h))
    kvspec = pl.BlockSpec((seq, HEAD_W), lambda b, h, i: (b, h))
    mspec = pl.BlockSpec((HEAD_W, HEAD_W), lambda b, h, i: (0, h))
    vec = lambda n: pl.BlockSpec((1, n), lambda b, h, i: (0, 0))
    return pl.pallas_call(
        functools.partial(_attn_kernel, tq=tq, lam_init=lam_init),
        grid=(nb, HEADS, nq),
        in_specs=[qspec, kvspec, kvspec, mspec, mspec, vec(DA_DH), vec(DA_DH), vec(DA_DH), vec(DA_DH),
                  vec(HEAD_W)],
        out_specs=qspec,
        out_shape=jax.ShapeDtypeStruct((nb * seq, GROUP_W), BF16),
        scratch_shapes=[pltpu.VMEM((2 * tq, HEAD_W), F32), pltpu.VMEM((2 * tq, 2 * HEAD_W), F32)],
        compiler_params=pltpu.CompilerParams(dimension_semantics=("parallel", "parallel", "arbitrary"),
                                             vmem_limit_bytes=VMEM_LIMIT),
        name="prompt_attn",
    )(q, k, v, k_meta, v_meta, *lams, subln)


def _decode_kernel(pt_ref, q_ref, kn_ref, vn_ref, lq1_ref, lk1_ref, lq2_ref, lk2_ref, subln_ref, *refs,
                   pages, n_new, lam_init):
    del pt_ref
    k_pages, v_pages = refs[:pages], refs[pages:2 * pages]
    o_ref, m_scr, acc_scr = refs[2 * pages:]
    s_idx = pl.program_id(1)
    rows = 2 * n_new

    @pl.when(s_idx == 0)
    def _():
        _softmax_init(m_scr, acc_scr)

    for h in range(HEADS):
        sl = slice(h * HEAD_W, (h + 1) * HEAD_W)
        kh = jnp.concatenate([r[sl, :] for r in k_pages], axis=1).astype(BF16)
        vh = jnp.concatenate([r[pl.ds(h, PAGE_SIZE, stride=HEADS), :] for r in v_pages], axis=0).astype(BF16)
        _softmax_step(_dot(q_ref[h], kh), vh, m_scr.at[h], acc_scr.at[h])

    @pl.when(s_idx == pl.num_programs(1) - 1)
    def _():
        lam = _lambda(lq1_ref[...], lk1_ref[...], lq2_ref[...], lk2_ref[...], lam_init)
        r = lax.broadcasted_iota(jnp.int32, (rows, HEAD_W), 0)
        c = lax.broadcasted_iota(jnp.int32, (rows, HEAD_W), 1)
        r = jnp.where(r >= n_new, r - n_new, r)
        visible = c <= r
        for h in range(HEADS):
            sl = slice(h * HEAD_W, (h + 1) * HEAD_W)
            s = _dot(q_ref[h], kn_ref[:, sl], NT)
            _softmax_step(jnp.where(visible, s, NEG_INF), vn_ref[:, sl], m_scr.at[h], acc_scr.at[h])
            out = _softmax_result(acc_scr[h])
            o = out - lam * pltpu.roll(out, rows - n_new, 0)
            o = _rms(o) * subln_ref[...] * (1.0 - lam_init)
            o_ref[:, sl] = o


def _decode_attn(page_table, q_split, k_new, v_new, cache_k, cache_v, lams, subln, *, pages, n_new, lam_init):
    nb, n_pages = page_table.shape
    steps = n_pages // pages
    rows = 2 * n_new
    per_b = lambda shape: pl.BlockSpec((None,) + shape, lambda b, s, pt: (b,) + (0,) * len(shape))
    vec = lambda n: pl.BlockSpec((1, n), lambda b, s, pt: (0, 0))

    def k_spec(p):
        return pl.BlockSpec((None, GROUP_W, PAGE_SIZE), lambda b, s, pt: (pt[b, s * pages + p], 0, 0))

    def v_spec(p):
        return pl.BlockSpec((None, PAGE_SIZE * HEADS, HEAD_W), lambda b, s, pt: (pt[b, s * pages + p], 0, 0))

    grid_spec = pltpu.PrefetchScalarGridSpec(
        num_scalar_prefetch=1,
        grid=(nb, steps),
        in_specs=[per_b((HEADS, rows, HEAD_W)), per_b((PAGE_SIZE, GROUP_W)), per_b((PAGE_SIZE, GROUP_W)),
                  vec(DA_DH), vec(DA_DH), vec(DA_DH), vec(DA_DH), vec(HEAD_W)]
                 + [k_spec(p) for p in range(pages)] + [v_spec(p) for p in range(pages)],
        out_specs=per_b((rows, GROUP_W)),
        scratch_shapes=[pltpu.VMEM((HEADS, rows, HEAD_W), F32), pltpu.VMEM((HEADS, rows, 2 * HEAD_W), F32)],
    )
    return pl.pallas_call(
        functools.partial(_decode_kernel, pages=pages, n_new=n_new, lam_init=lam_init),
        grid_spec=grid_spec,
        out_shape=jax.ShapeDtypeStruct((nb, rows, GROUP_W), F32),
        compiler_params=pltpu.CompilerParams(dimension_semantics=("parallel", "arbitrary"),
                                             vmem_limit_bytes=VMEM_LIMIT),
        name="decode_attn",
    )(page_table, q_split, k_new, v_new, *lams, subln, *([cache_k] * pages), *([cache_v] * pages))


def _mlp_kernel(x_ref, ohg_ref, oda_ref, wo_ref, lnf_ref, wu_ref, wd_ref, lnfin_ref, y_ref, *, ff_chunk):
    x = x_ref[...]
    x = x + _dot(ohg_ref[...], wo_ref[0:GROUP_W, :]) + _dot(oda_ref[...], wo_ref[GROUP_W:2 * GROUP_W, :])
    hn = (_rms(x) * lnf_ref[...]).astype(BF16)
    acc = jnp.zeros_like(x)
    for c in range(wu_ref.shape[1] // ff_chunk):
        cs = slice(c * ff_chunk, (c + 1) * ff_chunk)
        u = jnp.maximum(_dot(hn, wu_ref[:, cs]), 0.0)
        acc = acc + _dot((u * u).astype(BF16), wd_ref[cs, :])
    y_ref[...] = _rms(x + acc) * lnfin_ref[...]


def _mlp(x, ohg, oda, w_out, ln_ffn, w_up, w_down, ln_final, *, tm, ff_chunk=512):
    rows = x.shape[0]
    row = lambda w: pl.BlockSpec((tm, w), lambda i: (i, 0))
    return pl.pallas_call(
        functools.partial(_mlp_kernel, ff_chunk=ff_chunk),
        grid=(rows // tm,),
        in_specs=[row(D_MODEL), row(GROUP_W), row(GROUP_W), _const_spec(w_out.shape), _const_spec((1, D_MODEL)),
                  _const_spec(w_up.shape), _const_spec(w_down.shape), _const_spec((1, D_MODEL))],
        out_specs=row(D_MODEL),
        out_shape=jax.ShapeDtypeStruct((rows, D_MODEL), F32),
        compiler_params=pltpu.CompilerParams(dimension_semantics=("parallel",),
                                             vmem_limit_bytes=VMEM_LIMIT),
        name="mlp",
    )(x, ohg, oda, w_out, ln_ffn, w_up, w_down, ln_final)


def _rope_tables(pos):
    half = DA_DH // 2
    inv = jnp.power(ROPE_THETA, -jnp.arange(half, dtype=F32) * 2.0 / DA_DH)
    ang = pos.astype(F32)[:, None] * inv[None, :]
    cos, sin = jnp.cos(ang), jnp.sin(ang)
    return jnp.concatenate([cos] * 4, axis=1), jnp.concatenate([-sin, sin, -sin, sin], axis=1)


def _pad_rows(a, n, value=0.0):
    return jnp.pad(a, ((0, 0), (0, n - a.shape[1]), (0, 0)), constant_values=value)


def kernel(x_prompt, x_sample, cache_k, cache_v, state_hgrn, page_table, meta_tokens, ln_mix, w_in,
           hg_lb_logits, hg_norm, da_lambda_q1, da_lambda_k1, da_lambda_q2, da_lambda_k2, da_subln, w_out,
           ln_ffn, w_up, w_down, ln_final):
    nb_p, seq, _ = x_prompt.shape
    nb_s, n_new, _ = x_sample.shape
    depth = w_in.shape[0]
    assert depth == 1, "single-layer step"
    layer = 0
    past_len = page_table.shape[1] * PAGE_SIZE
    lam_init = 0.8 - 0.6 * math.exp(-0.3 * layer)

    w_in_l = w_in[layer].astype(BF16)
    w_out_l = w_out[layer].astype(BF16)
    w_up_l = w_up[layer].astype(BF16)
    w_down_l = w_down[layer].astype(BF16)
    ln_mix_l = ln_mix[layer][None, :]
    ln_ffn_l = ln_ffn[layer][None, :]
    ln_fin = ln_final[None, :]
    hg_norm_l = hg_norm[layer][None, :]
    subln_l = da_subln[layer][None, :]
    lams = [a[layer][None, :] for a in (da_lambda_q1, da_lambda_k1, da_lambda_q2, da_lambda_k2)]

    pos_s = past_len + jnp.arange(n_new, dtype=jnp.int32)
    pos_small = jnp.concatenate([jnp.arange(N_META, dtype=jnp.int32), jnp.tile(pos_s, nb_s)])
    cos_s, sin_s = _rope_tables(pos_small)
    xs = x_sample.reshape(nb_s * n_new, D_MODEL)
    x_small = jnp.concatenate([meta_tokens.astype(F32), xs], axis=0)
    small = _inproj(x_small, cos_s, sin_s, ln_mix_l, w_in_l, hg_lb_logits, tm=x_small.shape[0], layer=layer)
    qh_m, f_m, vh_m, gate_m, _, k16_m, v16_m, k32_m, v32_m = [a[:N_META] for a in small]
    qh_s, f_s, vh_s, gate_s, qr_s, k16_s, v16_s, k32_s, v32_s = [
        a[N_META:].reshape(nb_s, n_new, GROUP_W) for a in small]

    cos_p, sin_p = _rope_tables(N_META + jnp.arange(seq, dtype=jnp.int32))
    xp = x_prompt.reshape(nb_p * seq, D_MODEL)
    qh_p, f_p, vh_p, gate_p, qr_p, k16_p, v16_p, k_full, v_full = _inproj(
        xp, cos_p, sin_p, ln_mix_l, w_in_l, hg_lb_logits, tm=512, layer=layer,
        meta_kv=(k32_m, v32_m.reshape(N_META * HEADS, HEAD_W)), seq=seq)

    zero_state = jnp.zeros((1, HEADS, HEAD_W, HEAD_W), F32)
    _, s_meta = _hgrn(qh_m, f_m, vh_m, gate_m, zero_state, hg_norm_l, nb=1, chunk=N_META, tm=N_META,
                      out_dtype=F32)
    ohg_p, state_p = _hgrn(qh_p, f_p, vh_p, gate_p, s_meta, hg_norm_l, nb=nb_p, chunk=64, tm=512,
                           out_dtype=BF16)
    cs = 16
    pad = lambda a, val=0.0: _pad_rows(a.astype(F32), cs, val).reshape(nb_s * cs, GROUP_W)
    ohg_s, state_s = _hgrn(pad(qh_s), pad(f_s, 1.0), pad(vh_s), pad(gate_s), state_hgrn[layer], hg_norm_l,
                           nb=nb_s, chunk=cs, tm=cs, out_dtype=F32)
    ohg_s = ohg_s.reshape(nb_s, cs, GROUP_W)[:, :n_new].reshape(nb_s * n_new, GROUP_W)

    k_meta16 = jnp.pad(k16_m, ((0, HEAD_W - N_META), (0, 0)))
    v_meta16 = jnp.pad(v16_m, ((0, HEAD_W - N_META), (0, 0)))
    oda_p = _attn(qr_p, k16_p, v16_p, k_meta16, v_meta16, lams, subln_l, nb=nb_p, seq=seq, tq=512,
                  lam_init=lam_init)

    q4 = qr_s.reshape(nb_s, n_new, HEADS, HEAD_W).transpose(0, 2, 1, 3)
    lane = jnp.arange(HEAD_W) < DA_DH
    q_split = jnp.concatenate([jnp.where(lane, q4, 0), jnp.where(lane, 0, q4)], axis=2).astype(BF16)
    oda_s = _decode_attn(page_table, q_split, _pad_rows(k16_s, PAGE_SIZE), _pad_rows(v16_s, PAGE_SIZE),
                         jnp.transpose(cache_k[layer], (0, 2, 3, 4, 1)).reshape(-1, GROUP_W, PAGE_SIZE),
                         cache_v[layer].reshape(-1, PAGE_SIZE * HEADS, HEAD_W),
                         lams, subln_l, pages=8, n_new=n_new, lam_init=lam_init)
    oda_s = oda_s[:, :n_new].reshape(nb_s * n_new, GROUP_W)

    y_p = _mlp(xp, ohg_p, oda_p, w_out_l, ln_ffn_l, w_up_l, w_down_l, ln_fin, tm=512)
    y_s = _mlp(xs, ohg_s.astype(BF16), oda_s.astype(BF16), w_out_l, ln_ffn_l, w_up_l, w_down_l, ln_fin,
               tm=nb_s * n_new)

    return (y_p.reshape(nb_p, seq, D_MODEL),
            y_s.reshape(nb_s, n_new, D_MODEL),
            k_full.reshape(1, nb_p, N_META + seq, HEADS, 2, DA_DH),
            v_full.reshape(1, nb_p, N_META + seq, HEADS, HEAD_W),
            state_p[None],
            k32_s.reshape(1, nb_s, n_new, HEADS, 2, DA_DH),
            v32_s.reshape(1, nb_s, n_new, HEADS, HEAD_W),
            state_s[None])
```

```python
import functools
import math

import jax
import jax.numpy as jnp
import numpy as np
from jax import lax
from jax.experimental import pallas as pl
from jax.experimental.pallas import tpu as pltpu

F32 = jnp.float32
BF16 = jnp.bfloat16

D_MODEL = 1024
N_META = 16
HEADS = 4
HEAD_W = 128
GROUP_W = HEADS * HEAD_W
DA_DH = 64
PAGE_SIZE = 128
ROPE_THETA = 10000.0
EPS = 1e-6
NEG_INF = -1e30
N_SEG = 7

VMEM_LIMIT = 56 * 1024 * 1024

NT = (((1,), (1,)), ((), ()))
TN = (((0,), (0,)), ((), ()))


def _dot(a, b, dims=None):
    if dims is None:
        return jnp.dot(a, b, preferred_element_type=F32)
    return lax.dot_general(a, b, dims, preferred_element_type=F32)


def _rms(x):
    return x * lax.rsqrt(jnp.mean(x * x, axis=-1, keepdims=True) + EPS)


def _const_spec(shape):
    nd = len(shape)
    return pl.BlockSpec(shape, lambda *_: (0,) * nd, pipeline_mode=pl.Buffered(1))


def _inproj_rows(x_ref, cos_ref, sin_ref, ln_ref, w_ref, lb_ref,
                 qh_ref, f_ref, vh_ref, gate_ref, qr_ref, k16_ref, v16_ref, *, layer):
    x = x_ref[...]
    h = (_rms(x) * ln_ref[...]).astype(BF16)

    def seg(j):
        return _dot(h, w_ref[:, j * GROUP_W:(j + 1) * GROUP_W])

    lg = lb_ref[...]
    e = jnp.exp(lg - jnp.max(lg, axis=0, keepdims=True))
    sm = e / jnp.sum(e, axis=0, keepdims=True)
    lb = jnp.sum(sm[:layer + 1], axis=0, keepdims=True)

    qh_ref[...] = jax.nn.silu(seg(0)).astype(qh_ref.dtype)
    f_ref[...] = lb + (1.0 - lb) * jax.nn.sigmoid(seg(1))
    vh_ref[...] = seg(2).astype(vh_ref.dtype)
    gate_ref[...] = jax.nn.silu(seg(3)).astype(gate_ref.dtype)

    c4 = jnp.concatenate([cos_ref[...]] * HEADS, axis=1)
    s4 = jnp.concatenate([sin_ref[...]] * HEADS, axis=1)
    lane = lax.broadcasted_iota(jnp.int32, c4.shape, 1)
    first_half = (lane & (DA_DH // 2)) == 0

    def rope(y):
        swapped = jnp.where(first_half,
                            pltpu.roll(y, GROUP_W - DA_DH // 2, 1),
                            pltpu.roll(y, DA_DH // 2, 1))
        return y * c4 + swapped * s4

    qr_ref[...] = (rope(seg(4)) * (DA_DH ** -0.5 * math.log2(math.e))).astype(qr_ref.dtype)
    k = rope(seg(5))
    k16_ref[...] = k.astype(BF16)
    v = seg(6)
    v16_ref[...] = v.astype(BF16)
    return k, v


def _inproj_kernel(*refs, layer):
    k32_ref, v32_ref = refs[-2:]
    k32_ref[...], v32_ref[...] = _inproj_rows(*refs[:-2], layer=layer)


def _inproj_prompt_kernel(x_ref, cos_ref, sin_ref, ln_ref, w_ref, lb_ref, kmeta_ref, vmeta_ref,
                          qh_ref, f_ref, vh_ref, gate_ref, qr_ref, k16_ref, v16_ref, kfull_ref, vfull_ref,
                          kbuf, vbuf, sem, meta_sem, *, layer, tm, nblk, seq):
    g = pl.program_id(0)
    last = pl.num_programs(0) - 1
    per_prompt = N_META + seq

    def copies(step):
        row0 = (step // nblk) * per_prompt + N_META + (step % nblk) * tm
        s = step % 2
        return (pltpu.make_async_copy(kbuf.at[s], kfull_ref.at[pl.ds(row0, tm), :], sem.at[0, s]),
                pltpu.make_async_copy(vbuf.at[s], vfull_ref.at[pl.ds(row0 * HEADS, tm * HEADS), :], sem.at[1, s]))

    @pl.when(g >= 2)
    def _():
        for c in copies(g - 2):
            c.wait()

    k, v = _inproj_rows(x_ref, cos_ref, sin_ref, ln_ref, w_ref, lb_ref,
                        qh_ref, f_ref, vh_ref, gate_ref, qr_ref, k16_ref, v16_ref, layer=layer)
    slot = g % 2
    kbuf[slot] = k
    for h in range(HEADS):
        vbuf.at[slot][pl.ds(h, tm, stride=HEADS), :] = v[:, h * HEAD_W:(h + 1) * HEAD_W]
    for c in copies(g):
        c.start()

    @pl.when(g % nblk == 0)
    def _():
        base = (g // nblk) * per_prompt
        mk = pltpu.make_async_copy(kmeta_ref, kfull_ref.at[pl.ds(base, N_META), :], meta_sem.at[0])
        mv = pltpu.make_async_copy(vmeta_ref, vfull_ref.at[pl.ds(base * HEADS, N_META * HEADS), :], meta_sem.at[1])
        mk.start()
        mv.start()
        mk.wait()
        mv.wait()

    @pl.when(g == last)
    def _():
        for c in copies(g) + copies(g - 1):
            c.wait()


def _inproj(x, cos, sin, ln_mix, w_in, lb_logits, *, tm, layer, meta_kv=None, seq=None):
    rows = x.shape[0]
    nper = cos.shape[0] // tm
    grid = (rows // tm,)
    row = lambda w: pl.BlockSpec((tm, w), lambda i: (i, 0))
    tab = pl.BlockSpec((tm, HEAD_W), lambda i: (i % nper, 0))
    in_specs = [row(D_MODEL), tab, tab, _const_spec((1, D_MODEL)), _const_spec(w_in.shape),
                _const_spec(lb_logits.shape)]
    narrow = [BF16, F32, BF16, BF16, BF16, BF16, BF16]
    out_specs = [row(GROUP_W)] * len(narrow)
    out_shape = [jax.ShapeDtypeStruct((rows, GROUP_W), d) for d in narrow]
    if meta_kv is None:
        return pl.pallas_call(
            functools.partial(_inproj_kernel, layer=layer),
            grid=grid, in_specs=in_specs,
            out_specs=out_specs + [row(GROUP_W)] * 2,
            out_shape=out_shape + [jax.ShapeDtypeStruct((rows, GROUP_W), F32)] * 2,
            compiler_params=pltpu.CompilerParams(dimension_semantics=("parallel",),
                                                 vmem_limit_bytes=VMEM_LIMIT),
            name="inproj",
        )(x, cos, sin, ln_mix, w_in, lb_logits)
    nblk = seq // tm
    full_rows = (rows // seq) * (N_META + seq)
    any_spec = pl.BlockSpec(memory_space=pl.ANY)
    return pl.pallas_call(
        functools.partial(_inproj_prompt_kernel, layer=layer, tm=tm, nblk=nblk, seq=seq),
        grid=grid,
        in_specs=in_specs + [_const_spec(meta_kv[0].shape), _const_spec(meta_kv[1].shape)],
        out_specs=out_specs + [any_spec, any_spec],
        out_shape=out_shape + [jax.ShapeDtypeStruct((full_rows, GROUP_W), F32),
                               jax.ShapeDtypeStruct((full_rows * HEADS, HEAD_W), F32)],
        scratch_shapes=[pltpu.VMEM((2, tm, GROUP_W), F32), pltpu.VMEM((2, tm * HEADS, HEAD_W), F32),
                        pltpu.SemaphoreType.DMA((2, 2)), pltpu.SemaphoreType.DMA((2,))],
        compiler_params=pltpu.CompilerParams(dimension_semantics=("arbitrary",),
                                             vmem_limit_bytes=VMEM_LIMIT),
        name="inproj_prompt",
    )(x, cos, sin, ln_mix, w_in, lb_logits, *meta_kv)


def _hgrn_weights(chunk):
    levels = int(math.log2(chunk))
    w = np.zeros(((2 + levels) * chunk, chunk), np.float32)
    for t in range(chunk):
        w[t, :t + 1] = 1.0
        w[chunk + t, t + 1:] = 1.0
    row, half = 2 * chunk, chunk // 2
    while half >= 1:
        for t in range(chunk):
            blk = t // half
            if blk % 2 == 1:
                w[row + t, blk * half:t + 1] = 1.0
            else:
                w[row + t, t + 1:blk * half + half] = 1.0
        row += chunk
        half //= 2
    return w


def _hgrn_kernel(q_ref, f_ref, v_ref, gate_ref, s0_ref, w_ref, norm_ref, o_ref, s_out_ref, st_scr,
                 *, chunk, n_chunks):
    i = pl.program_id(1)
    levels = int(math.log2(chunk))

    @pl.when(i == 0)
    def _():
        for h in range(HEADS):
            st_scr[h] = s0_ref[h].T

    row = lax.broadcasted_iota(jnp.int32, (chunk, chunk), 0)
    col = lax.broadcasted_iota(jnp.int32, (chunk, chunk), 1)
    rowv = lax.broadcasted_iota(jnp.int32, (chunk, GROUP_W), 0)
    eye = row == col
    pair_masks, second_half = [], []
    half = chunk // 2
    for _ in range(levels):
        sh = int(math.log2(half))
        pair_masks.append((row >> (sh + 1)) == (col >> (sh + 1)))
        second_half.append(((rowv >> sh) & 1) == 1)
        half //= 2
    w = w_ref[...]
    norm = norm_ref[...]

    def body(c, carry):
        r0 = pl.multiple_of(c * chunk, chunk)
        rows = pl.ds(r0, chunk)
        q = q_ref[rows, :].astype(F32)
        f = f_ref[rows, :]
        v = v_ref[rows, :].astype(BF16)
        gate = gate_ref[rows, :].astype(F32)
        g = jnp.log(f)
        k = 1.0 - f
        g_hi = g.astype(BF16)
        g_lo = (g - g_hi.astype(F32)).astype(BF16)
        ex = _dot(w, jnp.concatenate([g_hi, g_lo], axis=0))
        eb = jnp.exp(ex[0:chunk])
        qe = (q * eb).astype(BF16)
        kk = (k * jnp.exp(ex[chunk:2 * chunk])).astype(BF16)
        qb = q.astype(BF16)
        kb = k.astype(BF16)
        qt, kt = [], []
        zeros = jnp.zeros_like(q)
        for l in range(levels):
            dec = jnp.exp(ex[(2 + l) * chunk:(3 + l) * chunk])
            half = chunk >> (l + 1)
            if half >= 8:
                qp, kp = [], []
                for blk in range(chunk // half):
                    rs = slice(blk * half, (blk + 1) * half)
                    odd = blk % 2 == 1
                    qp.append(q[rs] * dec[rs] if odd else zeros[rs])
                    kp.append(zeros[rs] if odd else k[rs] * dec[rs])
                qt.append(jnp.concatenate(qp, axis=0).astype(BF16))
                kt.append(jnp.concatenate(kp, axis=0).astype(BF16))
            else:
                qt.append(jnp.where(second_half[l], q * dec, 0.0).astype(BF16))
                kt.append(jnp.where(second_half[l], 0.0, k * dec).astype(BF16))
        eb_last = eb[chunk - 1:chunk, :]
        for h in range(HEADS):
            sl = slice(h * HEAD_W, (h + 1) * HEAD_W)
            a = jnp.where(eye, _dot(qb[:, sl], kb[:, sl], NT), 0.0)
            for l in range(levels):
                a = a + jnp.where(pair_masks[l], _dot(qt[l][:, sl], kt[l][:, sl], NT), 0.0)
            st = st_scr[h]
            o = _dot(a.astype(BF16), v[:, sl]) + _dot(qe[:, sl], st.astype(BF16), NT)
            st_scr[h] = st * eb_last[:, sl] + _dot(v[:, sl], kk[:, sl], TN)
            o = _rms(o) * norm * gate[:, sl]
            o_ref[rows, sl] = o.astype(o_ref.dtype)
        return carry

    lax.fori_loop(0, n_chunks, body, 0, unroll=4 if n_chunks % 4 == 0 else 1)

    @pl.when(i == pl.num_programs(1) - 1)
    def _():
        for h in range(HEADS):
            s_out_ref[h] = st_scr[h].T


def _hgrn(q, f, v, gate, s0, hg_norm, *, nb, chunk, tm, out_dtype):
    rows = q.shape[0]
    nblk = rows // nb // tm
    w = _hgrn_weights(chunk)
    w = jnp.asarray(np.concatenate([w, w], axis=1), BF16)
    per_batch_state = s0.shape[0] == nb
    row = pl.BlockSpec((tm, GROUP_W), lambda b, i: (b * nblk + i, 0))
    st_in = pl.BlockSpec((None, HEADS, HEAD_W, HEAD_W),
                         (lambda b, i: (b, 0, 0, 0)) if per_batch_state else (lambda b, i: (0, 0, 0, 0)))
    st_out = pl.BlockSpec((None, HEADS, HEAD_W, HEAD_W), lambda b, i: (b, 0, 0, 0))
    return pl.pallas_call(
        functools.partial(_hgrn_kernel, chunk=chunk, n_chunks=tm // chunk),
        grid=(nb, nblk),
        in_specs=[row, row, row, row, st_in, _const_spec(w.shape), _const_spec((1, HEAD_W))],
        out_specs=[row, st_out],
        out_shape=[jax.ShapeDtypeStruct((rows, GROUP_W), out_dtype),
                   jax.ShapeDtypeStruct((nb, HEADS, HEAD_W, HEAD_W), F32)],
        scratch_shapes=[pltpu.VMEM((HEADS, HEAD_W, HEAD_W), F32)],
        compiler_params=pltpu.CompilerParams(dimension_semantics=("parallel", "arbitrary"),
                                             vmem_limit_bytes=VMEM_LIMIT),
        name=f"hgrn_c{chunk}",
    )(q, f, v, gate, s0, w, hg_norm)


def _lambda(lq1, lk1, lq2, lk2, lam_init):
    return (jnp.exp(jnp.sum(lq1 * lk1, axis=-1, keepdims=True))
            - jnp.exp(jnp.sum(lq2 * lk2, axis=-1, keepdims=True)) + lam_init)


def _softmax_step(s, v, m_ref, acc_ref):
    alpha, p = _softmax_probs(s, m_ref)
    acc_ref[...] = _both_halves(alpha) * acc_ref[...] + _pv(p, v)


def _softmax_probs(s, m_ref, dtype=BF16):
    blocks = [s[:, j:j + HEAD_W] for j in range(0, s.shape[1], HEAD_W)]
    mx = functools.reduce(jnp.maximum, blocks)
    m_prev = m_ref[...]
    m_new = jnp.maximum(m_prev, jnp.max(mx, axis=-1, keepdims=True))
    m_ref[...] = m_new
    return jnp.exp2(m_prev - m_new), jnp.concatenate([jnp.exp2(b - m_new).astype(dtype) for b in blocks], axis=1)


def _pv(p, v):
    return _dot(p, jnp.concatenate([v, jnp.ones_like(v)], axis=1))


def _both_halves(alpha):
    return jnp.concatenate([alpha, alpha], axis=1)


def _softmax_init(m_ref, acc_ref):
    m_ref[...] = jnp.full(m_ref.shape, NEG_INF, F32)
    acc_ref[...] = jnp.zeros(acc_ref.shape, F32)


def _softmax_result(acc):
    return acc[:, :HEAD_W] / acc[:, HEAD_W:]


def _split_maps(q):
    lane = lax.broadcasted_iota(jnp.int32, q.shape, 1)
    zero = jnp.zeros_like(q)
    return jnp.concatenate([jnp.where(lane < DA_DH, q, zero), jnp.where(lane >= DA_DH, q, zero)], axis=0)


def _attn_kernel(q_ref, k_ref, v_ref, km_ref, vm_ref, lq1_ref, lk1_ref, lq2_ref, lk2_ref, subln_ref,
                 o_ref, m_scr, acc_scr, *, tq, lam_init):
    qi = pl.program_id(2)
    qbd = _split_maps(q_ref[...])
    col = lax.broadcasted_iota(jnp.int32, (2 * tq, HEAD_W), 1)

    _softmax_init(m_scr, acc_scr)

    def block(j):
        return pl.ds(pl.multiple_of(j * tq, tq), tq)

    def scores(j):
        return _dot(qbd, k_ref[block(j), :], NT)

    def values(j):
        return v_ref[block(j), :]

    def causal(s):
        r = lax.broadcasted_iota(jnp.int32, s.shape, 0)
        c = lax.broadcasted_iota(jnp.int32, s.shape, 1)
        return jnp.where(c <= jnp.where(r >= tq, r - tq, r), s, NEG_INF)

    def body(j, s):
        s_next = scores(j + 1)
        _softmax_step(s, values(j), m_scr, acc_scr)
        return s_next

    s = lax.fori_loop(0, qi, body, scores(0))
    _softmax_step(causal(s), values(qi), m_scr, acc_scr)

    s = _dot(qbd, km_ref[...], NT)
    _softmax_step(jnp.where(col < N_META, s, NEG_INF), vm_ref[...], m_scr, acc_scr)

    lam =_lambda(lq1_ref[...], lk1_ref[...], lq2_ref[...], lk2_ref[...], lam_init)
    out = _softmax_result(acc_scr[...])
    o = out[:tq] - lam * out[tq:]
    o = _rms(o) * subln_ref[...] * (1.0 - lam_init)
    o_ref[...] = o.astype(o_ref.dtype)


def _attn(q, k, v, k_meta, v_meta, lams, subln, *, nb, seq, tq, lam_init):
    nq = seq // tq
    qspec = pl.BlockSpec((tq, HEAD_W), lambda b, h, i: (b * nq + i, h))
    kvspec = pl.BlockSpec((seq, HEAD_W), lambda b, h, i: (b, h))
    mspec = pl.BlockSpec((HEAD_W, HEAD_W), lambda b, h, i: (0, h))
    vec = lambda n: pl.BlockSpec((1, n), lambda b, h, i: (0, 0))
    return pl.pallas_call(
        functools.partial(_attn_kernel, tq=tq, lam_init=lam_init),
        grid=(nb, HEADS, nq),
        in_specs=[qspec, kvspec, kvspec, mspec, mspec, vec(DA_DH), vec(DA_DH), vec(DA_DH), vec(DA_DH),
                  vec(HEAD_W)],
        out_specs=qspec,
        out_shape=jax.ShapeDtypeStruct((nb * seq, GROUP_W), BF16),
        scratch_shapes=[pltpu.VMEM((2 * tq, HEAD_W), F32), pltpu.VMEM((2 * tq, 2 * HEAD_W), F32)],
        compiler_params=pltpu.CompilerParams(dimension_semantics=("parallel", "parallel", "arbitrary"),
                                             vmem_limit_bytes=VMEM_LIMIT),
        name="prompt_attn",
    )(q, k, v, k_meta, v_meta, *lams, subln)


def _decode_kernel(pt_ref, q_ref, kn_ref, vn_ref, lq1_ref, lk1_ref, lq2_ref, lk2_ref, subln_ref, kc_ref, vc_ref,
                   o_ref, kbuf, vbuf, sem, m_scr, acc_scr, *, pages, n_new, lam_init, depth):
    s_idx = pl.program_id(1)
    n_steps = pl.num_programs(1)
    t = pl.program_id(0) * n_steps + s_idx
    total = pl.num_programs(0) * n_steps
    rows = 2 * n_new

    def page_copies(step, waiting=False):
        seq_i, step_i, slot = step // n_steps, step % n_steps, step % depth
        out = []
        for p in range(pages):
            page = 0 if waiting else pt_ref[seq_i, step_i * pages + p]
            out.append(pltpu.make_async_copy(kc_ref.at[page], kbuf.at[slot, p], sem.at[0, slot]))
            out.append(pltpu.make_async_copy(vc_ref.at[page], vbuf.at[slot, p], sem.at[1, slot]))
        return out

    @pl.when(t == 0)
    def _():
        for d in range(depth - 1):
            for c in page_copies(d):
                c.start()

    for c in page_copies(t, waiting=True):
        c.wait()

    @pl.when(s_idx == 0)
    def _():
        _softmax_init(m_scr, acc_scr)

    def update(s, head_values):
        alpha, p = _softmax_probs(s, m_scr, F32)
        pv = [_pv(p[h * rows:(h + 1) * rows].astype(BF16), head_values(h)) for h in range(HEADS)]
        acc_scr[...] = _both_halves(alpha) * acc_scr[...] + jnp.concatenate(pv, axis=0)

    slot = t % depth
    k_all = jnp.concatenate([kbuf[slot, p] for p in range(pages)], axis=1).astype(BF16)
    update(_dot(q_ref[...], k_all),
           lambda h: jnp.concatenate([vbuf[slot, p, pl.ds(h, PAGE_SIZE, stride=HEADS), :] for p in range(pages)],
                                     axis=0).astype(BF16))

    @pl.when(t + (depth - 1) < total)
    def _():
        for c in page_copies(t + (depth - 1)):
            c.start()

    @pl.when(s_idx == pl.num_programs(1) - 1)
    def _():
        lam = _lambda(lq1_ref[...], lk1_ref[...], lq2_ref[...], lk2_ref[...], lam_init)
        r = lax.broadcasted_iota(jnp.int32, (HEADS * rows, HEAD_W), 0)
        c = lax.broadcasted_iota(jnp.int32, (HEADS * rows, HEAD_W), 1)
        visible = c <= (r & (n_new - 1))
        s = _dot(q_ref[...], kn_ref[...], NT)
        update(jnp.where(visible, s, NEG_INF), lambda h: vn_ref[:, h * HEAD_W:(h + 1) * HEAD_W])
        out = _softmax_result(acc_scr[...])
        o = out - lam * pltpu.roll(out, HEADS * rows - n_new, 0)
        o = _rms(o) * subln_ref[...] * (1.0 - lam_init)
        for h in range(HEADS):
            o_ref[:, h * HEAD_W:(h + 1) * HEAD_W] = o[h * rows:(h + 1) * rows]


def _decode_attn(page_table, q_split, k_new, v_new, cache_k, cache_v, lams, subln, *, pages, n_new, lam_init):
    nb, n_pages = page_table.shape
    assert n_pages % pages == 0 and n_new & (n_new - 1) == 0
    steps = n_pages // pages
    rows = 2 * n_new
    depth = 3
    per_b = lambda shape: pl.BlockSpec((None,) + shape, lambda b, s, pt: (b,) + (0,) * len(shape))
    vec = lambda n: pl.BlockSpec((1, n), lambda b, s, pt: (0, 0))
    hbm = pl.BlockSpec(memory_space=pl.ANY)
    grid_spec = pltpu.PrefetchScalarGridSpec(
        num_scalar_prefetch=1,
        grid=(nb, steps),
        in_specs=[per_b((HEADS * rows, GROUP_W)), per_b((PAGE_SIZE, GROUP_W)), per_b((PAGE_SIZE, GROUP_W)),
                  vec(DA_DH), vec(DA_DH), vec(DA_DH), vec(DA_DH), vec(HEAD_W), hbm, hbm],
        out_specs=per_b((rows, GROUP_W)),
        scratch_shapes=[pltpu.VMEM((depth, pages) + cache_k.shape[1:], F32),
                        pltpu.VMEM((depth, pages) + cache_v.shape[1:], F32),
                        pltpu.SemaphoreType.DMA((2, depth)),
                        pltpu.VMEM((HEADS * rows, HEAD_W), F32), pltpu.VMEM((HEADS * rows, 2 * HEAD_W), F32)],
    )
    return pl.pallas_call(
        functools.partial(_decode_kernel, pages=pages, n_new=n_new, lam_init=lam_init, depth=depth),
        grid_spec=grid_spec,
        out_shape=jax.ShapeDtypeStruct((nb, rows, GROUP_W), F32),
        compiler_params=pltpu.CompilerParams(dimension_semantics=("arbitrary", "arbitrary"),
                                             vmem_limit_bytes=VMEM_LIMIT),
        name="decode_attn",
    )(page_table, q_split, k_new, v_new, *lams, subln, cache_k, cache_v)


def _mlp_kernel(x_ref, ohg_ref, oda_ref, wo_ref, lnf_ref, wu_ref, wd_ref, lnfin_ref, y_ref, *, ff_chunk):
    x = x_ref[...]
    x = x + _dot(ohg_ref[...], wo_ref[0:GROUP_W, :]) + _dot(oda_ref[...], wo_ref[GROUP_W:2 * GROUP_W, :])
    hn = (_rms(x) * lnf_ref[...]).astype(BF16)
    acc = jnp.zeros_like(x)
    for c in range(wu_ref.shape[1] // ff_chunk):
        cs = slice(c * ff_chunk, (c + 1) * ff_chunk)
        u = jnp.maximum(_dot(hn, wu_ref[:, cs]), 0.0)
        acc = acc + _dot((u * u).astype(BF16), wd_ref[cs, :])
    y_ref[...] = _rms(x + acc) * lnfin_ref[...]


def _mlp(x, ohg, oda, w_out, ln_ffn, w_up, w_down, ln_final, *, tm, ff_chunk=512):
    rows = x.shape[0]
    row = lambda w: pl.BlockSpec((tm, w), lambda i: (i, 0))
    return pl.pallas_call(
        functools.partial(_mlp_kernel, ff_chunk=ff_chunk),
        grid=(rows // tm,),
        in_specs=[row(D_MODEL), row(GROUP_W), row(GROUP_W), _const_spec(w_out.shape), _const_spec((1, D_MODEL)),
                  _const_spec(w_up.shape), _const_spec(w_down.shape), _const_spec((1, D_MODEL))],
        out_specs=row(D_MODEL),
        out_shape=jax.ShapeDtypeStruct((rows, D_MODEL), F32),
        compiler_params=pltpu.CompilerParams(dimension_semantics=("parallel",),
                                             vmem_limit_bytes=VMEM_LIMIT),
        name="mlp",
    )(x, ohg, oda, w_out, ln_ffn, w_up, w_down, ln_final)


def _rope_tables(pos):
    half = DA_DH // 2
    inv = jnp.power(ROPE_THETA, -jnp.arange(half, dtype=F32) * 2.0 / DA_DH)
    ang = pos.astype(F32)[:, None] * inv[None, :]
    cos, sin = jnp.cos(ang), jnp.sin(ang)
    return jnp.concatenate([cos] * 4, axis=1), jnp.concatenate([-sin, sin, -sin, sin], axis=1)


def _pad_rows(a, n, value=0.0):
    return jnp.pad(a, ((0, 0), (0, n - a.shape[1]), (0, 0)), constant_values=value)


def kernel(x_prompt, x_sample, cache_k, cache_v, state_hgrn, page_table, meta_tokens, ln_mix, w_in,
           hg_lb_logits, hg_norm, da_lambda_q1, da_lambda_k1, da_lambda_q2, da_lambda_k2, da_subln, w_out,
           ln_ffn, w_up, w_down, ln_final):
    nb_p, seq, _ = x_prompt.shape
    nb_s, n_new, _ = x_sample.shape
    depth = w_in.shape[0]
    assert depth == 1, "single-layer step"
    layer = 0
    past_len = page_table.shape[1] * PAGE_SIZE
    lam_init = 0.8 - 0.6 * math.exp(-0.3 * layer)

    w_in_l = w_in[layer].astype(BF16)
    w_out_l = w_out[layer].astype(BF16)
    w_up_l = w_up[layer].astype(BF16)
    w_down_l = w_down[layer].astype(BF16)
    ln_mix_l = ln_mix[layer][None, :]
    ln_ffn_l = ln_ffn[layer][None, :]
    ln_fin = ln_final[None, :]
    hg_norm_l = hg_norm[layer][None, :]
    subln_l = da_subln[layer][None, :]
    lams = [a[layer][None, :] for a in (da_lambda_q1, da_lambda_k1, da_lambda_q2, da_lambda_k2)]

    pos_s = past_len + jnp.arange(n_new, dtype=jnp.int32)
    pos_small = jnp.concatenate([jnp.arange(N_META, dtype=jnp.int32), jnp.tile(pos_s, nb_s)])
    cos_s, sin_s = _rope_tables(pos_small)
    xs = x_sample.reshape(nb_s * n_new, D_MODEL)
    x_small = jnp.concatenate([meta_tokens.astype(F32), xs], axis=0)
    small = _inproj(x_small, cos_s, sin_s, ln_mix_l, w_in_l, hg_lb_logits, tm=x_small.shape[0], layer=layer)
    qh_m, f_m, vh_m, gate_m, _, k16_m, v16_m, k32_m, v32_m = [a[:N_META] for a in small]
    qh_s, f_s, vh_s, gate_s, qr_s, k16_s, v16_s, k32_s, v32_s = [
        a[N_META:].reshape(nb_s, n_new, GROUP_W) for a in small]

    cos_p, sin_p = _rope_tables(N_META + jnp.arange(seq, dtype=jnp.int32))
    xp = x_prompt.reshape(nb_p * seq, D_MODEL)
    qh_p, f_p, vh_p, gate_p, qr_p, k16_p, v16_p, k_full, v_full = _inproj(
        xp, cos_p, sin_p, ln_mix_l, w_in_l, hg_lb_logits, tm=512, layer=layer,
        meta_kv=(k32_m, v32_m.reshape(N_META * HEADS, HEAD_W)), seq=seq)

    zero_state = jnp.zeros((1, HEADS, HEAD_W, HEAD_W), F32)
    _, s_meta = _hgrn(qh_m, f_m, vh_m, gate_m, zero_state, hg_norm_l, nb=1, chunk=N_META, tm=N_META,
                      out_dtype=F32)
    ohg_p, state_p = _hgrn(qh_p, f_p, vh_p, gate_p, s_meta, hg_norm_l, nb=nb_p, chunk=64, tm=512,
                           out_dtype=BF16)
    cs = 16
    pad = lambda a, val=0.0: _pad_rows(a.astype(F32), cs, val).reshape(nb_s * cs, GROUP_W)
    ohg_s, state_s = _hgrn(pad(qh_s), pad(f_s, 1.0), pad(vh_s), pad(gate_s), state_hgrn[layer], hg_norm_l,
                           nb=nb_s, chunk=cs, tm=cs, out_dtype=F32)
    ohg_s = ohg_s.reshape(nb_s, cs, GROUP_W)[:, :n_new].reshape(nb_s * n_new, GROUP_W)

    k_meta16 = jnp.pad(k16_m, ((0, HEAD_W - N_META), (0, 0)))
    v_meta16 = jnp.pad(v16_m, ((0, HEAD_W - N_META), (0, 0)))
    oda_p = _attn(qr_p, k16_p, v16_p, k_meta16, v_meta16, lams, subln_l, nb=nb_p, seq=seq, tq=512,
                  lam_init=lam_init)

    lane = jnp.arange(GROUP_W)
    keep = ((lane // HEAD_W)[None, None, :] == jnp.arange(HEADS)[:, None, None]) & \
           (((lane % HEAD_W) // DA_DH)[None, None, :] == jnp.arange(2)[None, :, None])
    q_split = jnp.where(keep[None, :, :, None, :], qr_s[:, None, None, :, :], 0)
    q_split = q_split.reshape(nb_s, HEADS * 2 * n_new, GROUP_W).astype(BF16)
    oda_s = _decode_attn(page_table, q_split, _pad_rows(k16_s, PAGE_SIZE), _pad_rows(v16_s, PAGE_SIZE),
                         jnp.transpose(cache_k[layer], (0, 2, 3, 4, 1)).reshape(-1, GROUP_W, PAGE_SIZE),
                         cache_v[layer].reshape(-1, PAGE_SIZE * HEADS, HEAD_W),
                         lams, subln_l, pages=8, n_new=n_new, lam_init=lam_init)
    oda_s = oda_s[:, :n_new].reshape(nb_s * n_new, GROUP_W)

    y_p = _mlp(xp, ohg_p, oda_p, w_out_l, ln_ffn_l, w_up_l, w_down_l, ln_fin, tm=512)
    y_s = _mlp(xs, ohg_s.astype(BF16), oda_s.astype(BF16), w_out_l, ln_ffn_l, w_up_l, w_down_l, ln_fin,
               tm=nb_s * n_new)

    k_t = lax.optimization_barrier(jnp.transpose(k_full.reshape(nb_p, N_META + seq, GROUP_W), (0, 2, 1)))
    k_prompt = jnp.transpose(k_t.reshape(nb_p, HEADS, 2, DA_DH, N_META + seq), (0, 4, 1, 2, 3))[None]
    return (y_p.reshape(nb_p, seq, D_MODEL),
            y_s.reshape(nb_s, n_new, D_MODEL),
            k_prompt,
            v_full.reshape(1, nb_p, N_META + seq, HEADS, HEAD_W),
            state_p[None],
            k32_s.reshape(1, nb_s, n_new, HEADS, 2, DA_DH),
            v32_s.reshape(1, nb_s, n_new, HEADS, HEAD_W),
            state_s[None])
```

```python
import functools
import math

import jax
import jax.numpy as jnp
import numpy as np
from jax import lax
from jax.experimental import pallas as pl
from jax.experimental.pallas import tpu as pltpu

F32 = jnp.float32
BF16 = jnp.bfloat16

D_MODEL = 1024
N_META = 16
HEADS = 4
HEAD_W = 128
GROUP_W = HEADS * HEAD_W
DA_DH = 64
PAGE_SIZE = 128
ROPE_THETA = 10000.0
EPS = 1e-6
NEG_INF = -1e30
N_SEG = 7

VMEM_LIMIT = 56 * 1024 * 1024

NT = (((1,), (1,)), ((), ()))
TN = (((0,), (0,)), ((), ()))


def _dot(a, b, dims=None):
    if dims is None:
        return jnp.dot(a, b, preferred_element_type=F32)
    return lax.dot_general(a, b, dims, preferred_element_type=F32)


def _rms(x):
    return x * lax.rsqrt(jnp.mean(x * x, axis=-1, keepdims=True) + EPS)


def _const_spec(shape):
    nd = len(shape)
    return pl.BlockSpec(shape, lambda *_: (0,) * nd, pipeline_mode=pl.Buffered(1))


def _inproj_rows(x_ref, cos_ref, sin_ref, ln_ref, w_ref, lb_ref,
                 qh_ref, f_ref, vh_ref, gate_ref, qr_ref, k16_ref, v16_ref, *, layer):
    x = x_ref[...]
    h = (_rms(x) * ln_ref[...]).astype(BF16)

    def seg(j):
        return _dot(h, w_ref[:, j * GROUP_W:(j + 1) * GROUP_W])

    lg = lb_ref[...]
    e = jnp.exp(lg - jnp.max(lg, axis=0, keepdims=True))
    sm = e / jnp.sum(e, axis=0, keepdims=True)
    lb = jnp.sum(sm[:layer + 1], axis=0, keepdims=True)

    qh_ref[...] = jax.nn.silu(seg(0)).astype(qh_ref.dtype)
    f_ref[...] = lb + (1.0 - lb) * jax.nn.sigmoid(seg(1))
    vh_ref[...] = seg(2).astype(vh_ref.dtype)
    gate_ref[...] = jax.nn.silu(seg(3)).astype(gate_ref.dtype)

    c4 = jnp.concatenate([cos_ref[...]] * HEADS, axis=1)
    s4 = jnp.concatenate([sin_ref[...]] * HEADS, axis=1)
    lane = lax.broadcasted_iota(jnp.int32, c4.shape, 1)
    first_half = (lane & (DA_DH // 2)) == 0

    def rope(y):
        swapped = jnp.where(first_half,
                            pltpu.roll(y, GROUP_W - DA_DH // 2, 1),
                            pltpu.roll(y, DA_DH // 2, 1))
        return y * c4 + swapped * s4

    qr_ref[...] = (rope(seg(4)) * (DA_DH ** -0.5 * math.log2(math.e))).astype(qr_ref.dtype)
    k = rope(seg(5))
    k16_ref[...] = k.astype(BF16)
    v = seg(6)
    v16_ref[...] = v.astype(BF16)
    return k, v


def _inproj_kernel(*refs, layer):
    k32_ref, v32_ref = refs[-2:]
    k32_ref[...], v32_ref[...] = _inproj_rows(*refs[:-2], layer=layer)


def _inproj_prompt_kernel(x_ref, cos_ref, sin_ref, ln_ref, w_ref, lb_ref, kmeta_ref, vmeta_ref,
                          qh_ref, f_ref, vh_ref, gate_ref, qr_ref, k16_ref, v16_ref, kfull_ref, vfull_ref,
                          kbuf, vbuf, sem, meta_sem, *, layer, tm, nblk, seq):
    g = pl.program_id(0)
    last = pl.num_programs(0) - 1
    per_prompt = N_META + seq

    def copies(step):
        row0 = (step // nblk) * per_prompt + N_META + (step % nblk) * tm
        s = step % 2
        return (pltpu.make_async_copy(kbuf.at[s], kfull_ref.at[pl.ds(row0, tm), :], sem.at[0, s]),
                pltpu.make_async_copy(vbuf.at[s], vfull_ref.at[pl.ds(row0 * HEADS, tm * HEADS), :], sem.at[1, s]))

    @pl.when(g >= 2)
    def _():
        for c in copies(g - 2):
            c.wait()

    k, v = _inproj_rows(x_ref, cos_ref, sin_ref, ln_ref, w_ref, lb_ref,
                        qh_ref, f_ref, vh_ref, gate_ref, qr_ref, k16_ref, v16_ref, layer=layer)
    slot = g % 2
    kbuf[slot] = k
    for h in range(HEADS):
        vbuf.at[slot][pl.ds(h, tm, stride=HEADS), :] = v[:, h * HEAD_W:(h + 1) * HEAD_W]
    for c in copies(g):
        c.start()

    @pl.when(g % nblk == 0)
    def _():
        base = (g // nblk) * per_prompt
        mk = pltpu.make_async_copy(kmeta_ref, kfull_ref.at[pl.ds(base, N_META), :], meta_sem.at[0])
        mv = pltpu.make_async_copy(vmeta_ref, vfull_ref.at[pl.ds(base * HEADS, N_META * HEADS), :], meta_sem.at[1])
        mk.start()
        mv.start()
        mk.wait()
        mv.wait()

    @pl.when(g == last)
    def _():
        for c in copies(g) + copies(g - 1):
            c.wait()


def _inproj(x, cos, sin, ln_mix, w_in, lb_logits, *, tm, layer, meta_kv=None, seq=None):
    rows = x.shape[0]
    nper = cos.shape[0] // tm
    grid = (rows // tm,)
    row = lambda w: pl.BlockSpec((tm, w), lambda i: (i, 0))
    tab = pl.BlockSpec((tm, HEAD_W), lambda i: (i % nper, 0))
    in_specs = [row(D_MODEL), tab, tab, _const_spec((1, D_MODEL)), _const_spec(w_in.shape),
                _const_spec(lb_logits.shape)]
    narrow = [BF16, F32, BF16, BF16, BF16, BF16, BF16]
    out_specs = [row(GROUP_W)] * len(narrow)
    out_shape = [jax.ShapeDtypeStruct((rows, GROUP_W), d) for d in narrow]
    if meta_kv is None:
        return pl.pallas_call(
            functools.partial(_inproj_kernel, layer=layer),
            grid=grid, in_specs=in_specs,
            out_specs=out_specs + [row(GROUP_W)] * 2,
            out_shape=out_shape + [jax.ShapeDtypeStruct((rows, GROUP_W), F32)] * 2,
            compiler_params=pltpu.CompilerParams(dimension_semantics=("parallel",),
                                                 vmem_limit_bytes=VMEM_LIMIT),
            name="inproj",
        )(x, cos, sin, ln_mix, w_in, lb_logits)
    nblk = seq // tm
    full_rows = (rows // seq) * (N_META + seq)
    any_spec = pl.BlockSpec(memory_space=pl.ANY)
    return pl.pallas_call(
        functools.partial(_inproj_prompt_kernel, layer=layer, tm=tm, nblk=nblk, seq=seq),
        grid=grid,
        in_specs=in_specs + [_const_spec(meta_kv[0].shape), _const_spec(meta_kv[1].shape)],
        out_specs=out_specs + [any_spec, any_spec],
        out_shape=out_shape + [jax.ShapeDtypeStruct((full_rows, GROUP_W), F32),
                               jax.ShapeDtypeStruct((full_rows * HEADS, HEAD_W), F32)],
        scratch_shapes=[pltpu.VMEM((2, tm, GROUP_W), F32), pltpu.VMEM((2, tm * HEADS, HEAD_W), F32),
                        pltpu.SemaphoreType.DMA((2, 2)), pltpu.SemaphoreType.DMA((2,))],
        compiler_params=pltpu.CompilerParams(dimension_semantics=("arbitrary",),
                                             vmem_limit_bytes=VMEM_LIMIT),
        name="inproj_prompt",
    )(x, cos, sin, ln_mix, w_in, lb_logits, *meta_kv)


def _hgrn_weights(chunk):
    levels = int(math.log2(chunk))
    w = np.zeros(((2 + levels) * chunk, chunk), np.float32)
    for t in range(chunk):
        w[t, :t + 1] = 1.0
        w[chunk + t, t + 1:] = 1.0
    row, half = 2 * chunk, chunk // 2
    while half >= 1:
        for t in range(chunk):
            blk = t // half
            if blk % 2 == 1:
                w[row + t, blk * half:t + 1] = 1.0
            else:
                w[row + t, t + 1:blk * half + half] = 1.0
        row += chunk
        half //= 2
    return w


def _hgrn_kernel(q_ref, f_ref, v_ref, gate_ref, s0_ref, w_ref, norm_ref, o_ref, s_out_ref, st_scr,
                 *, chunk, n_chunks):
    i = pl.program_id(1)
    levels = int(math.log2(chunk))

    @pl.when(i == 0)
    def _():
        for h in range(HEADS):
            st_scr[h] = s0_ref[h].T

    row = lax.broadcasted_iota(jnp.int32, (chunk, chunk), 0)
    col = lax.broadcasted_iota(jnp.int32, (chunk, chunk), 1)
    rowv = lax.broadcasted_iota(jnp.int32, (chunk, GROUP_W), 0)
    eye = row == col
    pair_masks, second_half = [], []
    half = chunk // 2
    for _ in range(levels):
        sh = int(math.log2(half))
        pair_masks.append((row >> (sh + 1)) == (col >> (sh + 1)))
        second_half.append(((rowv >> sh) & 1) == 1)
        half //= 2
    w = w_ref[...]
    norm = norm_ref[...]

    def body(c, carry):
        r0 = pl.multiple_of(c * chunk, chunk)
        rows = pl.ds(r0, chunk)
        q = q_ref[rows, :].astype(F32)
        f = f_ref[rows, :]
        v = v_ref[rows, :].astype(BF16)
        gate = gate_ref[rows, :].astype(F32)
        g = jnp.log(f)
        k = 1.0 - f
        g_hi = g.astype(BF16)
        g_lo = (g - g_hi.astype(F32)).astype(BF16)
        ex = _dot(w, jnp.concatenate([g_hi, g_lo], axis=0))
        eb = jnp.exp(ex[0:chunk])
        qe = (q * eb).astype(BF16)
        kk = (k * jnp.exp(ex[chunk:2 * chunk])).astype(BF16)
        qb = q.astype(BF16)
        kb = k.astype(BF16)
        qt, kt = [], []
        zeros = jnp.zeros_like(q)
        for l in range(levels):
            dec = jnp.exp(ex[(2 + l) * chunk:(3 + l) * chunk])
            half = chunk >> (l + 1)
            if half >= 8:
                qp, kp = [], []
                for blk in range(chunk // half):
                    rs = slice(blk * half, (blk + 1) * half)
                    odd = blk % 2 == 1
                    qp.append(q[rs] * dec[rs] if odd else zeros[rs])
                    kp.append(zeros[rs] if odd else k[rs] * dec[rs])
                qt.append(jnp.concatenate(qp, axis=0).astype(BF16))
                kt.append(jnp.concatenate(kp, axis=0).astype(BF16))
            else:
                qt.append(jnp.where(second_half[l], q * dec, 0.0).astype(BF16))
                kt.append(jnp.where(second_half[l], 0.0, k * dec).astype(BF16))
        eb_last = eb[chunk - 1:chunk, :]
        for h in range(HEADS):
            sl = slice(h * HEAD_W, (h + 1) * HEAD_W)
            a = jnp.where(eye, _dot(qb[:, sl], kb[:, sl], NT), 0.0)
            for l in range(levels):
                a = a + jnp.where(pair_masks[l], _dot(qt[l][:, sl], kt[l][:, sl], NT), 0.0)
            st = st_scr[h]
            o = _dot(a.astype(BF16), v[:, sl]) + _dot(qe[:, sl], st.astype(BF16), NT)
            st_scr[h] = st * eb_last[:, sl] + _dot(v[:, sl], kk[:, sl], TN)
            o = _rms(o) * norm * gate[:, sl]
            o_ref[rows, sl] = o.astype(o_ref.dtype)
        return carry

    lax.fori_loop(0, n_chunks, body, 0, unroll=4 if n_chunks % 4 == 0 else 1)

    @pl.when(i == pl.num_programs(1) - 1)
    def _():
        for h in range(HEADS):
            s_out_ref[h] = st_scr[h].T


def _hgrn(q, f, v, gate, s0, hg_norm, *, nb, chunk, tm, out_dtype):
    rows = q.shape[0]
    nblk = rows // nb // tm
    w = _hgrn_weights(chunk)
    w = jnp.asarray(np.concatenate([w, w], axis=1), BF16)
    per_batch_state = s0.shape[0] == nb
    row = pl.BlockSpec((tm, GROUP_W), lambda b, i: (b * nblk + i, 0))
    st_in = pl.BlockSpec((None, HEADS, HEAD_W, HEAD_W),
                         (lambda b, i: (b, 0, 0, 0)) if per_batch_state else (lambda b, i: (0, 0, 0, 0)))
    st_out = pl.BlockSpec((None, HEADS, HEAD_W, HEAD_W), lambda b, i: (b, 0, 0, 0))
    return pl.pallas_call(
        functools.partial(_hgrn_kernel, chunk=chunk, n_chunks=tm // chunk),
        grid=(nb, nblk),
        in_specs=[row, row, row, row, st_in, _const_spec(w.shape), _const_spec((1, HEAD_W))],
        out_specs=[row, st_out],
        out_shape=[jax.ShapeDtypeStruct((rows, GROUP_W), out_dtype),
                   jax.ShapeDtypeStruct((nb, HEADS, HEAD_W, HEAD_W), F32)],
        scratch_shapes=[pltpu.VMEM((HEADS, HEAD_W, HEAD_W), F32)],
        compiler_params=pltpu.CompilerParams(dimension_semantics=("parallel", "arbitrary"),
                                             vmem_limit_bytes=VMEM_LIMIT),
        name=f"hgrn_c{chunk}",
    )(q, f, v, gate, s0, w, hg_norm)


def _lambda(lq1, lk1, lq2, lk2, lam_init):
    return (jnp.exp(jnp.sum(lq1 * lk1, axis=-1, keepdims=True))
            - jnp.exp(jnp.sum(lq2 * lk2, axis=-1, keepdims=True)) + lam_init)


def _softmax_step(s, v, m_ref, acc_ref, row_max=None):
    alpha, p = _softmax_probs(s, m_ref, row_max=row_max)
    acc_ref[...] = _both_halves(alpha) * acc_ref[...] + _pv(p, v)


def _lane_blocks(s):
    return [s[:, j:j + HEAD_W] for j in range(0, s.shape[1], HEAD_W)]


def _row_max(s):
    mx = jnp.max(functools.reduce(jnp.maximum, _lane_blocks(s)), axis=-1, keepdims=True)
    return jnp.broadcast_to(mx, (s.shape[0], HEAD_W))


def _softmax_probs(s, m_ref, dtype=BF16, row_max=None):
    m_prev = m_ref[...]
    m_new = jnp.maximum(m_prev, _row_max(s) if row_max is None else row_max)
    m_ref[...] = m_new
    return (jnp.exp2(m_prev - m_new),
            jnp.concatenate([jnp.exp2(b - m_new).astype(dtype) for b in _lane_blocks(s)], axis=1))


def _pv(p, v):
    return _dot(p, jnp.concatenate([v, jnp.ones_like(v)], axis=1))


def _both_halves(alpha):
    return jnp.concatenate([alpha, alpha], axis=1)


def _softmax_init(m_ref, acc_ref):
    m_ref[...] = jnp.full(m_ref.shape, NEG_INF, F32)
    acc_ref[...] = jnp.zeros(acc_ref.shape, F32)


def _softmax_result(acc):
    return acc[:, :HEAD_W] / acc[:, HEAD_W:]


def _split_maps(q):
    lane = lax.broadcasted_iota(jnp.int32, q.shape, 1)
    zero = jnp.zeros_like(q)
    return jnp.concatenate([jnp.where(lane < DA_DH, q, zero), jnp.where(lane >= DA_DH, q, zero)], axis=0)


def _attn_kernel(q_ref, k_ref, v_ref, km_ref, vm_ref, lq1_ref, lk1_ref, lq2_ref, lk2_ref, subln_ref,
                 o_ref, m_scr, acc_scr, s_scr, rm_scr, *, tq, lam_init):
    qi = pl.program_id(2)
    qbd = _split_maps(q_ref[...])
    col = lax.broadcasted_iota(jnp.int32, (2 * tq, HEAD_W), 1)

    _softmax_init(m_scr, acc_scr)

    def block(j):
        return pl.ds(pl.multiple_of(j * tq, tq), tq)

    def scores(j):
        return _dot(qbd, k_ref[block(j), :], NT)

    def values(j):
        return v_ref[block(j), :]

    def causal(s):
        r = lax.broadcasted_iota(jnp.int32, s.shape, 0)
        c = lax.broadcasted_iota(jnp.int32, s.shape, 1)
        return jnp.where(c <= jnp.where(r >= tq, r - tq, r), s, NEG_INF)

    def stage(j, slot):
        s = scores(j)
        s_scr[slot] = s
        rm_scr[slot] = _row_max(s)

    def update(j, slot):
        _softmax_step(s_scr[slot], values(j), m_scr, acc_scr, rm_scr[slot])

    def diagonal(slot):
        _softmax_step(causal(s_scr[slot]), values(qi), m_scr, acc_scr)

    def body(jj, carry):
        j = 2 * jj
        stage(j + 1, 1)
        update(j, 0)
        stage(j + 2, 0)
        update(j + 1, 1)
        return carry

    stage(0, 0)
    lax.fori_loop(0, qi // 2, body, 0)

    @pl.when(qi % 2 == 1)
    def _():
        stage(qi, 1)
        update(qi - 1, 0)
        diagonal(1)

    @pl.when(qi % 2 == 0)
    def _():
        diagonal(0)

    s = _dot(qbd, km_ref[...], NT)
    _softmax_step(jnp.where(col < N_META, s, NEG_INF), vm_ref[...], m_scr, acc_scr)

    lam =_lambda(lq1_ref[...], lk1_ref[...], lq2_ref[...], lk2_ref[...], lam_init)
    out = _softmax_result(acc_scr[...])
    o = out[:tq] - lam * out[tq:]
    o = _rms(o) * subln_ref[...] * (1.0 - lam_init)
    o_ref[...] = o.astype(o_ref.dtype)


def _attn(q, k, v, k_meta, v_meta, lams, subln, *, nb, seq, tq, lam_init):
    nq = seq // tq
    qspec = pl.BlockSpec((tq, HEAD_W), lambda b, h, i: (b * nq + i, h))
    kvspec = pl.BlockSpec((seq, HEAD_W), lambda b, h, i: (b, h))
    mspec = pl.BlockSpec((HEAD_W, HEAD_W), lambda b, h, i: (0, h))
    vec = lambda n: pl.BlockSpec((1, n), lambda b, h, i: (0, 0))
    return pl.pallas_call(
        functools.partial(_attn_kernel, tq=tq, lam_init=lam_init),
        grid=(nb, HEADS, nq),
        in_specs=[qspec, kvspec, kvspec, mspec, mspec, vec(DA_DH), vec(DA_DH), vec(DA_DH), vec(DA_DH),
                  vec(HEAD_W)],
        out_specs=qspec,
        out_shape=jax.ShapeDtypeStruct((nb * seq, GROUP_W), BF16),
        scratch_shapes=[pltpu.VMEM((2 * tq, HEAD_W), F32), pltpu.VMEM((2 * tq, 2 * HEAD_W), F32),
                        pltpu.VMEM((2, 2 * tq, tq), F32), pltpu.VMEM((2, 2 * tq, HEAD_W), F32)],
        compiler_params=pltpu.CompilerParams(dimension_semantics=("parallel", "parallel", "arbitrary"),
                                             vmem_limit_bytes=VMEM_LIMIT),
        name="prompt_attn",
    )(q, k, v, k_meta, v_meta, *lams, subln)


def _decode_kernel(pt_ref, q_ref, kn_ref, vn_ref, lq1_ref, lk1_ref, lq2_ref, lk2_ref, subln_ref, kc_ref, vc_ref,
                   o_ref, kbuf, vbuf, sem, m_scr, acc_scr, *, pages, n_new, lam_init, depth):
    s_idx = pl.program_id(1)
    n_steps = pl.num_programs(1)
    t = pl.program_id(0) * n_steps + s_idx
    total = pl.num_programs(0) * n_steps
    rows = 2 * n_new

    def page_copies(step, waiting=False):
        seq_i, step_i, slot = step // n_steps, step % n_steps, step % depth
        out = []
        for p in range(pages):
            page = 0 if waiting else pt_ref[seq_i, step_i * pages + p]
            out.append(pltpu.make_async_copy(kc_ref.at[page], kbuf.at[slot, p], sem.at[0, slot]))
            out.append(pltpu.make_async_copy(vc_ref.at[page], vbuf.at[slot, p], sem.at[1, slot]))
        return out

    @pl.when(t == 0)
    def _():
        for d in range(depth - 1):
            for c in page_copies(d):
                c.start()

    for c in page_copies(t, waiting=True):
        c.wait()

    @pl.when(s_idx == 0)
    def _():
        _softmax_init(m_scr, acc_scr)

    def update(s, head_values):
        alpha, p = _softmax_probs(s, m_scr, F32)
        pv = [_pv(p[h * rows:(h + 1) * rows].astype(BF16), head_values(h)) for h in range(HEADS)]
        acc_scr[...] = _both_halves(alpha) * acc_scr[...] + jnp.concatenate(pv, axis=0)

    slot = t % depth
    k_all = jnp.concatenate([kbuf[slot, p] for p in range(pages)], axis=1).astype(BF16)
    update(_dot(q_ref[...], k_all),
           lambda h: jnp.concatenate([vbuf[slot, p, pl.ds(h, PAGE_SIZE, stride=HEADS), :] for p in range(pages)],
                                     axis=0).astype(BF16))

    @pl.when(t + (depth - 1) < total)
    def _():
        for c in page_copies(t + (depth - 1)):
            c.start()

    @pl.when(s_idx == pl.num_programs(1) - 1)
    def _():
        lam = _lambda(lq1_ref[...], lk1_ref[...], lq2_ref[...], lk2_ref[...], lam_init)
        r = lax.broadcasted_iota(jnp.int32, (HEADS * rows, HEAD_W), 0)
        c = lax.broadcasted_iota(jnp.int32, (HEADS * rows, HEAD_W), 1)
        visible = c <= (r & (n_new - 1))
        s = _dot(q_ref[...], kn_ref[...], NT)
        update(jnp.where(visible, s, NEG_INF), lambda h: vn_ref[:, h * HEAD_W:(h + 1) * HEAD_W])
        out = _softmax_result(acc_scr[...])
        o = out - lam * pltpu.roll(out, HEADS * rows - n_new, 0)
        o = _rms(o) * subln_ref[...] * (1.0 - lam_init)
        for h in range(HEADS):
            o_ref[:, h * HEAD_W:(h + 1) * HEAD_W] = o[h * rows:(h + 1) * rows]


def _decode_attn(page_table, q_split, k_new, v_new, cache_k, cache_v, lams, subln, *, pages, n_new, lam_init):
    nb, n_pages = page_table.shape
    assert n_pages % pages == 0 and n_new & (n_new - 1) == 0
    steps = n_pages // pages
    rows = 2 * n_new
    depth = 3
    per_b = lambda shape: pl.BlockSpec((None,) + shape, lambda b, s, pt: (b,) + (0,) * len(shape))
    vec = lambda n: pl.BlockSpec((1, n), lambda b, s, pt: (0, 0))
    hbm = pl.BlockSpec(memory_space=pl.ANY)
    grid_spec = pltpu.PrefetchScalarGridSpec(
        num_scalar_prefetch=1,
        grid=(nb, steps),
        in_specs=[per_b((HEADS * rows, GROUP_W)), per_b((PAGE_SIZE, GROUP_W)), per_b((PAGE_SIZE, GROUP_W)),
                  vec(DA_DH), vec(DA_DH), vec(DA_DH), vec(DA_DH), vec(HEAD_W), hbm, hbm],
        out_specs=per_b((rows, GROUP_W)),
        scratch_shapes=[pltpu.VMEM((depth, pages) + cache_k.shape[1:], F32),
                        pltpu.VMEM((depth, pages) + cache_v.shape[1:], F32),
                        pltpu.SemaphoreType.DMA((2, depth)),
                        pltpu.VMEM((HEADS * rows, HEAD_W), F32), pltpu.VMEM((HEADS * rows, 2 * HEAD_W), F32)],
    )
    return pl.pallas_call(
        functools.partial(_decode_kernel, pages=pages, n_new=n_new, lam_init=lam_init, depth=depth),
        grid_spec=grid_spec,
        out_shape=jax.ShapeDtypeStruct((nb, rows, GROUP_W), F32),
        compiler_params=pltpu.CompilerParams(dimension_semantics=("arbitrary", "arbitrary"),
                                             vmem_limit_bytes=VMEM_LIMIT),
        name="decode_attn",
    )(page_table, q_split, k_new, v_new, *lams, subln, cache_k, cache_v)


def _mlp_kernel(x_ref, ohg_ref, oda_ref, wo_ref, lnf_ref, wu_ref, wd_ref, lnfin_ref, y_ref, *, ff_chunk):
    x = x_ref[...]
    x = x + _dot(ohg_ref[...], wo_ref[0:GROUP_W, :]) + _dot(oda_ref[...], wo_ref[GROUP_W:2 * GROUP_W, :])
    hn = (_rms(x) * lnf_ref[...]).astype(BF16)
    acc = jnp.zeros_like(x)
    for c in range(wu_ref.shape[1] // ff_chunk):
        cs = slice(c * ff_chunk, (c + 1) * ff_chunk)
        u = jnp.maximum(_dot(hn, wu_ref[:, cs]), 0.0)
        acc = acc + _dot((u * u).astype(BF16), wd_ref[cs, :])
    y_ref[...] = _rms(x + acc) * lnfin_ref[...]


def _mlp(x, ohg, oda, w_out, ln_ffn, w_up, w_down, ln_final, *, tm, ff_chunk=512):
    rows = x.shape[0]
    row = lambda w: pl.BlockSpec((tm, w), lambda i: (i, 0))
    return pl.pallas_call(
        functools.partial(_mlp_kernel, ff_chunk=ff_chunk),
        grid=(rows // tm,),
        in_specs=[row(D_MODEL), row(GROUP_W), row(GROUP_W), _const_spec(w_out.shape), _const_spec((1, D_MODEL)),
                  _const_spec(w_up.shape), _const_spec(w_down.shape), _const_spec((1, D_MODEL))],
        out_specs=row(D_MODEL),
        out_shape=jax.ShapeDtypeStruct((rows, D_MODEL), F32),
        compiler_params=pltpu.CompilerParams(dimension_semantics=("parallel",),
                                             vmem_limit_bytes=VMEM_LIMIT),
        name="mlp",
    )(x, ohg, oda, w_out, ln_ffn, w_up, w_down, ln_final)


def _rope_tables(pos):
    half = DA_DH // 2
    inv = jnp.power(ROPE_THETA, -jnp.arange(half, dtype=F32) * 2.0 / DA_DH)
    ang = pos.astype(F32)[:, None] * inv[None, :]
    cos, sin = jnp.cos(ang), jnp.sin(ang)
    return jnp.concatenate([cos] * 4, axis=1), jnp.concatenate([-sin, sin, -sin, sin], axis=1)


def _pad_rows(a, n, value=0.0):
    return jnp.pad(a, ((0, 0), (0, n - a.shape[1]), (0, 0)), constant_values=value)


def kernel(x_prompt, x_sample, cache_k, cache_v, state_hgrn, page_table, meta_tokens, ln_mix, w_in,
           hg_lb_logits, hg_norm, da_lambda_q1, da_lambda_k1, da_lambda_q2, da_lambda_k2, da_subln, w_out,
           ln_ffn, w_up, w_down, ln_final):
    nb_p, seq, _ = x_prompt.shape
    nb_s, n_new, _ = x_sample.shape
    depth = w_in.shape[0]
    assert depth == 1, "single-layer step"
    layer = 0
    past_len = page_table.shape[1] * PAGE_SIZE
    lam_init = 0.8 - 0.6 * math.exp(-0.3 * layer)

    w_in_l = w_in[layer].astype(BF16)
    w_out_l = w_out[layer].astype(BF16)
    w_up_l = w_up[layer].astype(BF16)
    w_down_l = w_down[layer].astype(BF16)
    ln_mix_l = ln_mix[layer][None, :]
    ln_ffn_l = ln_ffn[layer][None, :]
    ln_fin = ln_final[None, :]
    hg_norm_l = hg_norm[layer][None, :]
    subln_l = da_subln[layer][None, :]
    lams = [a[layer][None, :] for a in (da_lambda_q1, da_lambda_k1, da_lambda_q2, da_lambda_k2)]

    pos_s = past_len + jnp.arange(n_new, dtype=jnp.int32)
    pos_small = jnp.concatenate([jnp.arange(N_META, dtype=jnp.int32), jnp.tile(pos_s, nb_s)])
    cos_s, sin_s = _rope_tables(pos_small)
    xs = x_sample.reshape(nb_s * n_new, D_MODEL)
    x_small = jnp.concatenate([meta_tokens.astype(F32), xs], axis=0)
    small = _inproj(x_small, cos_s, sin_s, ln_mix_l, w_in_l, hg_lb_logits, tm=x_small.shape[0], layer=layer)
    qh_m, f_m, vh_m, gate_m, _, k16_m, v16_m, k32_m, v32_m = [a[:N_META] for a in small]
    qh_s, f_s, vh_s, gate_s, qr_s, k16_s, v16_s, k32_s, v32_s = [
        a[N_META:].reshape(nb_s, n_new, GROUP_W) for a in small]

    cos_p, sin_p = _rope_tables(N_META + jnp.arange(seq, dtype=jnp.int32))
    xp = x_prompt.reshape(nb_p * seq, D_MODEL)
    qh_p, f_p, vh_p, gate_p, qr_p, k16_p, v16_p, k_full, v_full = _inproj(
        xp, cos_p, sin_p, ln_mix_l, w_in_l, hg_lb_logits, tm=512, layer=layer,
        meta_kv=(k32_m, v32_m.reshape(N_META * HEADS, HEAD_W)), seq=seq)

    zero_state = jnp.zeros((1, HEADS, HEAD_W, HEAD_W), F32)
    _, s_meta = _hgrn(qh_m, f_m, vh_m, gate_m, zero_state, hg_norm_l, nb=1, chunk=N_META, tm=N_META,
                      out_dtype=F32)
    ohg_p, state_p = _hgrn(qh_p, f_p, vh_p, gate_p, s_meta, hg_norm_l, nb=nb_p, chunk=64, tm=512,
                           out_dtype=BF16)
    cs = 16
    pad = lambda a, val=0.0: _pad_rows(a.astype(F32), cs, val).reshape(nb_s * cs, GROUP_W)
    ohg_s, state_s = _hgrn(pad(qh_s), pad(f_s, 1.0), pad(vh_s), pad(gate_s), state_hgrn[layer], hg_norm_l,
                           nb=nb_s, chunk=cs, tm=cs, out_dtype=F32)
    ohg_s = ohg_s.reshape(nb_s, cs, GROUP_W)[:, :n_new].reshape(nb_s * n_new, GROUP_W)

    k_meta16 = jnp.pad(k16_m, ((0, HEAD_W - N_META), (0, 0)))
    v_meta16 = jnp.pad(v16_m, ((0, HEAD_W - N_META), (0, 0)))
    oda_p = _attn(qr_p, k16_p, v16_p, k_meta16, v_meta16, lams, subln_l, nb=nb_p, seq=seq, tq=512,
                  lam_init=lam_init)

    lane = jnp.arange(GROUP_W)
    keep = ((lane // HEAD_W)[None, None, :] == jnp.arange(HEADS)[:, None, None]) & \
           (((lane % HEAD_W) // DA_DH)[None, None, :] == jnp.arange(2)[None, :, None])
    q_split = jnp.where(keep[None, :, :, None, :], qr_s[:, None, None, :, :], 0)
    q_split = q_split.reshape(nb_s, HEADS * 2 * n_new, GROUP_W).astype(BF16)
    oda_s = _decode_attn(page_table, q_split, _pad_rows(k16_s, PAGE_SIZE), _pad_rows(v16_s, PAGE_SIZE),
                         jnp.transpose(cache_k[layer], (0, 2, 3, 4, 1)).reshape(-1, GROUP_W, PAGE_SIZE),
                         cache_v[layer].reshape(-1, PAGE_SIZE * HEADS, HEAD_W),
                         lams, subln_l, pages=16, n_new=n_new, lam_init=lam_init)
    oda_s = oda_s[:, :n_new].reshape(nb_s * n_new, GROUP_W)

    y_p = _mlp(xp, ohg_p, oda_p, w_out_l, ln_ffn_l, w_up_l, w_down_l, ln_fin, tm=512)
    y_s = _mlp(xs, ohg_s.astype(BF16), oda_s.astype(BF16), w_out_l, ln_ffn_l, w_up_l, w_down_l, ln_fin,
               tm=nb_s * n_new)

    k_t = lax.optimization_barrier(jnp.transpose(k_full.reshape(nb_p, N_META + seq, GROUP_W), (0, 2, 1)))
    k_prompt = jnp.transpose(k_t.reshape(nb_p, HEADS, 2, DA_DH, N_META + seq), (0, 4, 1, 2, 3))[None]
    return (y_p.reshape(nb_p, seq, D_MODEL),
            y_s.reshape(nb_s, n_new, D_MODEL),
            k_prompt,
            v_full.reshape(1, nb_p, N_META + seq, HEADS, HEAD_W),
            state_p[None],
            k32_s.reshape(1, nb_s, n_new, HEADS, 2, DA_DH),
            v32_s.reshape(1, nb_s, n_new, HEADS, HEAD_W),
            state_s[None])
```
